```python
import math
import jax, jax.numpy as jnp
from jax import lax
import numpy as np

D_MODEL = 1024
BATCH = 2
SEQ = 8192
DEPTH = 4
DEC_BATCH = 128
DEC_SEQ = 4
PAST_LEN = 8192
PAGE_SIZE = 128

N_META = 16
D_RNN = D_MODEL
RNN_BLOCKS = 8
RNN_BLK = D_RNN // RNN_BLOCKS
RNN_CONV = 4
RGLRU_C = 8.0
N_HEADS = 8
QK_NOPE = 128
QK_ROPE = 64
V_DIM = 128
Q_RANK = 768
KV_RANK = 256
ROPE_THETA = 10000.0
D_FF = 3 * D_MODEL
FFN_CONV = 3
Q_BLOCK = 128
EPS = 1e-6
ATTN_SCALE = 1.0 / math.sqrt(QK_NOPE + QK_ROPE)
IN_WIDTH = 2 * D_RNN + Q_RANK + KV_RANK + QK_ROPE + 2 * D_MODEL
IN_SPLITS = [D_RNN, 2 * D_RNN, 2 * D_RNN + Q_RANK, 2 * D_RNN + Q_RANK + KV_RANK,
             2 * D_RNN + Q_RANK + KV_RANK + QK_ROPE, 2 * D_RNN + Q_RANK + KV_RANK + QK_ROPE + D_MODEL]

kernel_name = "hawk_mla_meta_convffn_step"


def rms_norm(x, g):
    xf = x.astype(jnp.float32)
    y = xf * lax.rsqrt(jnp.mean(xf * xf, axis=-1, keepdims=True) + EPS)
    return (y * g.astype(jnp.float32)).astype(x.dtype)


def causal_dwconv(x, buf, w, b):
    k = w.shape[0]
    t = x.shape[1]
    xp = jnp.concatenate([buf.astype(x.dtype), x], axis=1)
    y = b + w[0] * xp[:, 0:t]
    for j in range(1, k):
        y = y + w[j] * xp[:, j:j + t]
    return y, xp[:, t:]


def rope(x, pos):
    half = QK_ROPE // 2
    inv = ROPE_THETA ** (-jnp.arange(half, dtype=jnp.float32) / half)
    ang = pos.astype(jnp.float32)[:, None] * inv[None, :]
    cos = jnp.cos(ang)[:, None, :]
    sin = jnp.sin(ang)[:, None, :]
    xf = x.astype(jnp.float32)
    x1, x2 = xf[..., :half], xf[..., half:]
    return jnp.concatenate([x1 * cos - x2 * sin, x2 * cos + x1 * sin], axis=-1).astype(x.dtype)


def rglru(x, h0, wa, ba, wx, bx, lam):
    b_, t, _ = x.shape
    xb = x.reshape(b_, t, RNN_BLOCKS, RNN_BLK)
    r = jax.nn.sigmoid((jnp.einsum("btni,nij->btnj", xb, wa).reshape(b_, t, D_RNN) + ba).astype(jnp.float32))
    i = jax.nn.sigmoid((jnp.einsum("btni,nij->btnj", xb, wx).reshape(b_, t, D_RNN) + bx).astype(jnp.float32))
    log_a = -RGLRU_C * r * jax.nn.softplus(-lam.astype(jnp.float32))
    a = jnp.exp(log_a)
    u = jnp.sqrt(-jnp.expm1(2.0 * log_a)) * (i * x.astype(jnp.float32))
    u = u.at[:, 0].add(a[:, 0] * h0.astype(jnp.float32))

    def combine(left, right):
        a1, b1 = left
        a2, b2 = right
        return a1 * a2, a2 * b1 + b2

    _, h = lax.associative_scan(combine, (a, u), axis=1)
    return h, h[:, -1]


def mla_scores(q_lat, q_rope, c_kv, k_rope):
    s = jnp.einsum("bthr,bsr->bhts", q_lat, c_kv, preferred_element_type=jnp.float32)
    s = s + jnp.einsum("bthp,bsp->bhts", q_rope, k_rope, preferred_element_type=jnp.float32)
    return s * ATTN_SCALE


def mla_prompt_attend(q_lat, q_rope, c_kv, k_rope, w_uv):
    b_, l, h, r = q_lat.shape
    lp = -(-l // Q_BLOCK) * Q_BLOCK
    nb = lp // Q_BLOCK
    pad = ((0, 0), (0, lp - l), (0, 0), (0, 0))
    qb = jnp.pad(q_lat, pad).reshape(b_, nb, Q_BLOCK, h, r).transpose(1, 0, 2, 3, 4)
    qrb = jnp.pad(q_rope, pad).reshape(b_, nb, Q_BLOCK, h, QK_ROPE).transpose(1, 0, 2, 3, 4)
    starts = jnp.arange(nb, dtype=jnp.int32) * Q_BLOCK
    key_pos = jnp.arange(l, dtype=jnp.int32)

    def block(args):
        ql, qr, start = args
        s = mla_scores(ql, qr, c_kv, k_rope)
        qpos = start + jnp.arange(Q_BLOCK, dtype=jnp.int32)
        s = jnp.where(key_pos[None, :] <= qpos[:, None], s, -jnp.inf)
        p = jax.nn.softmax(s, axis=-1).astype(c_kv.dtype)
        return jnp.einsum("bhts,bsr->bthr", p, c_kv)

    o_lat = lax.map(block, (qb, qrb, starts))
    o_lat = o_lat.transpose(1, 0, 2, 3, 4).reshape(b_, lp, h, r)[:, :l]
    return jnp.einsum("bthr,rhv->bthv", o_lat, w_uv).reshape(b_, l, h * V_DIM)


def mla_sample_attend(past_c, past_r, q_lat, q_rope, c_kv, k_rope, w_uv):
    b_, t, h, _ = q_lat.shape
    n_past = past_c.shape[1]
    s_past = mla_scores(q_lat, q_rope, past_c, past_r)
    s_new = mla_scores(q_lat, q_rope, c_kv, k_rope)
    s_new = jnp.where(jnp.tril(jnp.ones((t, t), dtype=bool)), s_new, -jnp.inf)
    p = jax.nn.softmax(jnp.concatenate([s_past, s_new], axis=-1), axis=-1).astype(c_kv.dtype)
    o_lat = jnp.einsum("bhts,bsr->bthr", p[..., :n_past], past_c) + jnp.einsum("bhts,bsr->bthr", p[..., n_past:], c_kv)
    return jnp.einsum("bthr,rhv->bthv", o_lat, w_uv).reshape(b_, t, h * V_DIM)


def trunk_layer(x, pos, h0, rnn_buf, ffn_buf, attend, p):
    b_, t, _ = x.shape
    hn = rms_norm(x, p["norm_mix"])
    x_rnn, g_rnn, c_q, c_kv, k_r, g_a, g_b = jnp.split(hn @ p["w_in"], IN_SPLITS, axis=-1)
    xc, rnn_buf_new = causal_dwconv(x_rnn, rnn_buf, p["rnn_conv_w"], p["rnn_conv_b"])
    hs, h_last = rglru(xc, h0, p["rnn_wa"], p["rnn_ba"], p["rnn_wx"], p["rnn_bx"], p["rnn_lambda"])
    out_a = (hs.astype(x.dtype) * jax.nn.gelu(g_rnn)) @ p["w_br_rnn"]
    q = (rms_norm(c_q, p["q_norm"]) @ p["w_uq"]).reshape(b_, t, N_HEADS, QK_NOPE + QK_ROPE)
    q_nope, q_rope = q[..., :QK_NOPE], rope(q[..., QK_NOPE:], pos)
    q_lat = jnp.einsum("bthn,rhn->bthr", q_nope, p["w_uk"])
    c_kv_n = rms_norm(c_kv, p["kv_norm"])
    k_rope = rope(k_r[:, :, None, :], pos)[:, :, 0]
    out_b = attend(q_lat, q_rope, c_kv_n, k_rope, p["w_uv"]) @ p["w_br_attn"]
    x = x + (jax.nn.sigmoid(g_a) * out_a + jax.nn.sigmoid(g_b) * out_b) @ p["w_out"]
    hf = rms_norm(x, p["norm_ffn"])
    u, v = jnp.split(hf @ p["ffn_w_up"], 2, axis=-1)
    uc, ffn_buf_new = causal_dwconv(u, ffn_buf, p["ffn_conv_w"], p["ffn_conv_b"])
    x = x + (jax.nn.gelu(uc) * v) @ p["ffn_w_down"]
    return x, c_kv_n, k_rope, h_last, rnn_buf_new, ffn_buf_new


def setup_inputs(seed: int = 0) -> dict:
    key = jax.random.key(seed)
    ks = iter(jax.random.split(key, 48))

    def nrm(shape, scale):
        return jax.random.normal(next(ks), shape, jnp.float32) * scale

    n_pages = PAST_LEN // PAGE_SIZE
    n_used = DEC_BATCH * n_pages
    n_pool = (n_used * 5) // 4
    page_table = jax.random.permutation(next(ks), n_pool)[:n_used].reshape(DEC_BATCH, n_pages).astype(jnp.int32)
    a_c = jax.random.uniform(next(ks), (DEPTH, D_RNN), jnp.float32, 0.9, 0.999)
    s = a_c ** (1.0 / RGLRU_C)
    rnn_lambda = jnp.log(s) - jnp.log1p(-s)
    return {
        "x_prompt": nrm((BATCH, SEQ, D_MODEL), 1.0),
        "x_sample": nrm((DEC_BATCH, DEC_SEQ, D_MODEL), 1.0),
        "cache_kv_latent": nrm((DEPTH, n_pool, PAGE_SIZE, KV_RANK), 1.0),
        "cache_k_rope": nrm((DEPTH, n_pool, PAGE_SIZE, QK_ROPE), 1.0),
        "state_rnn_h": nrm((DEPTH, DEC_BATCH, D_RNN), 0.5),
        "state_rnn_conv": nrm((DEPTH, DEC_BATCH, RNN_CONV - 1, D_RNN), 1.0),
        "state_ffn_conv": nrm((DEPTH, DEC_BATCH, FFN_CONV - 1, D_FF), 1.0),
        "page_table": page_table,
        "meta_tokens": nrm((N_META, D_MODEL), 1.0),
        "norm_mix": 1.0 + nrm((DEPTH, D_MODEL), 0.05),
        "w_in": nrm((DEPTH, D_MODEL, IN_WIDTH), D_MODEL ** -0.5),
        "rnn_conv_w": nrm((DEPTH, RNN_CONV, D_RNN), RNN_CONV ** -0.5),
        "rnn_conv_b": nrm((DEPTH, D_RNN), 0.02),
        "rnn_wa": nrm((DEPTH, RNN_BLOCKS, RNN_BLK, RNN_BLK), RNN_BLK ** -0.5),
        "rnn_ba": nrm((DEPTH, D_RNN), 0.1),
        "rnn_wx": nrm((DEPTH, RNN_BLOCKS, RNN_BLK, RNN_BLK), RNN_BLK ** -0.5),
        "rnn_bx": nrm((DEPTH, D_RNN), 0.1),
        "rnn_lambda": rnn_lambda,
        "q_norm": 1.0 + nrm((DEPTH, Q_RANK), 0.05),
        "w_uq": nrm((DEPTH, Q_RANK, N_HEADS * (QK_NOPE + QK_ROPE)), Q_RANK ** -0.5),
        "kv_norm": 1.0 + nrm((DEPTH, KV_RANK), 0.05),
        "w_uk": nrm((DEPTH, KV_RANK, N_HEADS, QK_NOPE), KV_RANK ** -0.5),
        "w_uv": nrm((DEPTH, KV_RANK, N_HEADS, V_DIM), KV_RANK ** -0.5),
        "w_br_rnn": nrm((DEPTH, D_RNN, D_MODEL), D_RNN ** -0.5),
        "w_br_attn": nrm((DEPTH, N_HEADS * V_DIM, D_MODEL), (N_HEADS * V_DIM) ** -0.5),
        "w_out": nrm((DEPTH, D_MODEL, D_MODEL), D_MODEL ** -0.5),
        "norm_ffn": 1.0 + nrm((DEPTH, D_MODEL), 0.05),
        "ffn_w_up": nrm((DEPTH, D_MODEL, 2 * D_FF), D_MODEL ** -0.5),
        "ffn_conv_w": nrm((DEPTH, FFN_CONV, D_FF), FFN_CONV ** -0.5),
        "ffn_conv_b": nrm((DEPTH, D_FF), 0.02),
        "ffn_w_down": nrm((DEPTH, D_FF, D_MODEL), D_FF ** -0.5),
        "norm_final": 1.0 + nrm((D_MODEL,), 0.05),
    }


def reference(x_prompt, x_sample, cache_kv_latent, cache_k_rope, state_rnn_h, state_rnn_conv, state_ffn_conv,
              page_table, meta_tokens, norm_mix, w_in, rnn_conv_w, rnn_conv_b, rnn_wa, rnn_ba, rnn_wx, rnn_bx,
              rnn_lambda, q_norm, w_uq, kv_norm, w_uk, w_uv, w_br_rnn, w_br_attn, w_out, norm_ffn, ffn_w_up,
              ffn_conv_w, ffn_conv_b, ffn_w_down, norm_final):
    bp = x_prompt.shape[0]
    bs, t_s, _ = x_sample.shape
    page = cache_kv_latent.shape[2]
    past_len = page_table.shape[1] * page
    xp = jnp.concatenate([jnp.broadcast_to(meta_tokens[None].astype(x_prompt.dtype), (bp, N_META, D_MODEL)), x_prompt], axis=1)
    pos_p = jnp.arange(xp.shape[1], dtype=jnp.int32)
    xs = x_sample
    pos_s = past_len + jnp.arange(t_s, dtype=jnp.int32)

    p_lat, p_rope, p_h, p_rc, p_fc = [], [], [], [], []
    s_lat, s_rope, s_h, s_rc, s_fc = [], [], [], [], []
    for l in range(DEPTH):
        prm = {"norm_mix": norm_mix[l], "w_in": w_in[l], "rnn_conv_w": rnn_conv_w[l], "rnn_conv_b": rnn_conv_b[l],
               "rnn_wa": rnn_wa[l], "rnn_ba": rnn_ba[l], "rnn_wx": rnn_wx[l], "rnn_bx": rnn_bx[l],
               "rnn_lambda": rnn_lambda[l], "q_norm": q_norm[l], "w_uq": w_uq[l], "kv_norm": kv_norm[l],
               "w_uk": w_uk[l], "w_uv": w_uv[l], "w_br_rnn": w_br_rnn[l], "w_br_attn": w_br_attn[l],
               "w_out": w_out[l], "norm_ffn": norm_ffn[l], "ffn_w_up": ffn_w_up[l], "ffn_conv_w": ffn_conv_w[l],
               "ffn_conv_b": ffn_conv_b[l], "ffn_w_down": ffn_w_down[l]}
        xp, c_p, r_p, h_p, rb_p, fb_p = trunk_layer(
            xp, pos_p, jnp.zeros((bp, D_RNN), jnp.float32), jnp.zeros((bp, RNN_CONV - 1, D_RNN), xp.dtype),
            jnp.zeros((bp, FFN_CONV - 1, D_FF), xp.dtype), mla_prompt_attend, prm)
        past_c = cache_kv_latent[l, page_table].reshape(bs, past_len, KV_RANK)
        past_r = cache_k_rope[l, page_table].reshape(bs, past_len, QK_ROPE)
        attend_s = lambda ql, qr, ck, kr, wuv, pc=past_c, pr=past_r: mla_sample_attend(pc, pr, ql, qr, ck, kr, wuv)
        xs, c_s, r_s, h_s, rb_s, fb_s = trunk_layer(
            xs, pos_s, state_rnn_h[l], state_rnn_conv[l], state_ffn_conv[l], attend_s, prm)
        p_lat.append(c_p); p_rope.append(r_p); p_h.append(h_p); p_rc.append(rb_p); p_fc.append(fb_p)
        s_lat.append(c_s); s_rope.append(r_s); s_h.append(h_s); s_rc.append(rb_s); s_fc.append(fb_s)

    y_prompt = rms_norm(xp, norm_final)[:, N_META:]
    y_sample = rms_norm(xs, norm_final)
    return (y_prompt, y_sample,
            jnp.stack(p_lat), jnp.stack(p_rope), jnp.stack(p_h), jnp.stack(p_rc), jnp.stack(p_fc),
            jnp.stack(s_lat), jnp.stack(s_rope), jnp.stack(s_h), jnp.stack(s_rc), jnp.stack(s_fc))
```

```python
import functools
import math

import jax
import jax.numpy as jnp
from jax import lax
from jax.experimental import pallas as pl
from jax.experimental.pallas import tpu as pltpu

F32 = jnp.float32
BF16 = jnp.bfloat16

EPS = 1e-6
RGLRU_C = 8.0
ROPE_THETA = 10000.0
LANES = 128
SUBLANES = 8
VMEM_LIMIT = 56 * 1024 * 1024
PROMPT_TILE = 256
KEYS_PER_STEP = 1024


def _round_up(x, m):
    return (x + m - 1) // m * m


def _rms(x, g):
    return x * lax.rsqrt(jnp.mean(x * x, axis=-1, keepdims=True) + EPS) * g


def _gelu(x):
    return 0.5 * x * (1.0 + jnp.tanh(math.sqrt(2.0 / math.pi) * (x + 0.044715 * (x * x * x))))


def _sigmoid(x):
    return 1.0 / (1.0 + jnp.exp(-x))


def _dot(a, b):
    return jnp.dot(a, b, preferred_element_type=F32)


def _dot_nt(a, b):
    return lax.dot_general(a, b, (((1,), (1,)), ((), ())), preferred_element_type=F32)


def _params(*sem):
    return pltpu.CompilerParams(dimension_semantics=sem, vmem_limit_bytes=VMEM_LIMIT)


def _row_spec(tile, cols):
    return pl.BlockSpec((None, tile, cols), lambda g, t: (g, t, 0))


def _layer_spec(layer, *shape):
    zeros = (0,) * len(shape)
    return pl.BlockSpec((None,) + tuple(shape), lambda g, t: (layer,) + zeros)


def _in_proj_kernel(x_ref, nm_ref, w_ref, qn_ref, kvn_ref, wuq_ref, wuk_ref, cos_ref, sin_ref,
                    xrnn_ref, grnn_ref, ga_ref, gb_ref, ckv_ref, ckvb_ref, kr_ref, krb_ref,
                    qlat_ref, qrope_ref, *, d, q_rank, kv_rank, heads, nope, rope):
    h = _rms(x_ref[...], nm_ref[...]).astype(BF16)

    def proj(c0, c1):
        return _dot(h, w_ref[:, c0:c1])

    xrnn_ref[...] = proj(0, d)
    grnn_ref[...] = proj(d, 2 * d)
    ga_ref[...] = proj(2 * d, 3 * d)
    gb_ref[...] = proj(3 * d, 4 * d)
    o = 4 * d
    c_q = proj(o, o + q_rank)
    o += q_rank
    c_kv = _rms(proj(o, o + kv_rank), kvn_ref[...])
    o += kv_rank
    ckv_ref[...] = c_kv
    ckvb_ref[...] = c_kv.astype(BF16)
    cos = cos_ref[...]
    sin = sin_ref[...]
    kk = proj(o, o + 2 * rope)
    k_rot = (kk * cos + pltpu.roll(kk, rope, 1) * sin)[:, :rope]
    kr_ref[...] = k_rot
    krb_ref[...] = k_rot.astype(BF16)

    cqn = _rms(c_q, qn_ref[...]).astype(BF16)
    hn = heads * nope
    hr = heads * rope
    for j in range(hr // LANES):
        qr = _dot(cqn, wuq_ref[:, hn + j * LANES:hn + (j + 1) * LANES])
        qp = _dot(cqn, wuq_ref[:, hn + hr + j * LANES:hn + hr + (j + 1) * LANES])
        qrope_ref[:, j * LANES:(j + 1) * LANES] = (qr * cos + qp * sin).astype(BF16)
    for hd in range(heads):
        qn = _dot(cqn, wuq_ref[:, hd * nope:(hd + 1) * nope]).astype(BF16)
        qlat_ref[:, hd * kv_rank:(hd + 1) * kv_rank] = _dot(qn, wuk_ref[hd]).astype(BF16)


def _in_proj(x, cos4, sin4, wts, layer, tile, dims):
    g_, r_, d = x.shape
    q_rank, kv_rank, heads, nope, rope = dims
    width = wts["w_in"].shape[-1]
    grid = (g_, r_ // tile)
    f = functools.partial(_in_proj_kernel, d=d, q_rank=q_rank, kv_rank=kv_rank, heads=heads,
                          nope=nope, rope=rope)
    tab = pl.BlockSpec((tile, LANES), lambda g, t: (t, 0))
    out_shape = (
        jax.ShapeDtypeStruct((g_, r_, d), F32), jax.ShapeDtypeStruct((g_, r_, d), F32),
        jax.ShapeDtypeStruct((g_, r_, d), F32), jax.ShapeDtypeStruct((g_, r_, d), F32),
        jax.ShapeDtypeStruct((g_, r_, kv_rank), F32), jax.ShapeDtypeStruct((g_, r_, kv_rank), BF16),
        jax.ShapeDtypeStruct((g_, r_, rope), F32), jax.ShapeDtypeStruct((g_, r_, rope), BF16),
        jax.ShapeDtypeStruct((g_, r_, heads * kv_rank), BF16),
        jax.ShapeDtypeStruct((g_, r_, heads * rope), BF16),
    )
    out_specs = (
        _row_spec(tile, d), _row_spec(tile, d), _row_spec(tile, d), _row_spec(tile, d),
        _row_spec(tile, kv_rank), _row_spec(tile, kv_rank), _row_spec(tile, rope), _row_spec(tile, rope),
        _row_spec(tile, heads * kv_rank), _row_spec(tile, heads * rope),
    )
    in_specs = [
        _row_spec(tile, d), _layer_spec(layer, 1, d), _layer_spec(layer, d, width),
        _layer_spec(layer, 1, q_rank), _layer_spec(layer, 1, kv_rank),
        _layer_spec(layer, q_rank, wts["w_uq"].shape[-1]), _layer_spec(layer, heads, nope, kv_rank),
        tab, tab,
    ]
    return pl.pallas_call(
        f, grid=grid, in_specs=in_specs, out_specs=out_specs, out_shape=out_shape,
        compiler_params=_params("parallel", "parallel"), name="in_proj",
    )(x, wts["norm_mix"], wts["w_in"], wts["q_norm"], wts["kv_norm"], wts["w_uq"], wts["w_uk"], cos4, sin4)


def _scan_rows(a, u, step):
    n = a.shape[0]
    row = lax.broadcasted_iota(jnp.int32, a.shape, 0)
    d = step
    while d < n:
        keep = row >= d
        a_prev = jnp.where(keep, pltpu.roll(a, d, 0), 1.0)
        u_prev = jnp.where(keep, pltpu.roll(u, d, 0), 0.0)
        u = a * u_prev + u
        a = a * a_prev
        d *= 2
    return a, u


def _rnn_kernel(x_ref, g_ref, st_ref, h0_ref, cw_ref, cb_ref, wg_ref, ba_ref, bx_ref, lam_ref, wbr_ref,
                out_ref, hlast_ref, xbuf, hcar, gated, *, tile, pad, step, kconv, blk, last_tile, last_off):
    t = pl.program_id(1)

    @pl.when(t == 0)
    def _():
        xbuf[0:pad, :] = st_ref[...]
        hcar[...] = h0_ref[...]

    xbuf[pad:pad + tile, :] = x_ref[...]
    d = x_ref.shape[-1]
    for n in range(d // blk):
        cs = slice(n * blk, (n + 1) * blk)
        xc = cb_ref[:, cs] + cw_ref[kconv - 1:kconv, cs] * xbuf[pad:pad + tile, cs]
        for k in range(kconv - 1):
            off = pad - (kconv - 1 - k) * step
            xc = xc + cw_ref[k:k + 1, cs] * xbuf[off:off + tile, cs]
        gates = _dot(xc.astype(BF16), wg_ref[n])
        r = _sigmoid(gates[:, :blk] + ba_ref[:, cs])
        i = _sigmoid(gates[:, blk:] + bx_ref[:, cs])
        neg_lam = -lam_ref[:, cs]
        softplus = jnp.maximum(neg_lam, 0.0) + jnp.log(1.0 + jnp.exp(-jnp.abs(neg_lam)))
        log_a = -RGLRU_C * r * softplus
        a = jnp.exp(log_a)
        u = jnp.sqrt(1.0 - jnp.exp(2.0 * log_a)) * (i * xc)
        a_p, u_p = _scan_rows(a, u, step)
        h_in = hcar[:, cs]
        if step > 1:
            h_in = jnp.concatenate([h_in] * (tile // step), axis=0)
        hs = a_p * h_in + u_p
        hcar[:, cs] = hs[tile - step:tile, :]

        @pl.when(t == last_tile)
        def _():
            hlast_ref[:, cs] = hs[last_off:last_off + step, :]

        gated[:, cs] = (hs * _gelu(g_ref[:, cs])).astype(BF16)
    out_ref[...] = _dot(gated[...], wbr_ref[...])
    xbuf[0:pad, :] = xbuf[tile:tile + pad, :]


def _rnn(x_rnn, g_rnn, conv_state, h0, wts, layer, tile, step, t_real):
    g_, r_, d = x_rnn.shape
    kconv = wts["rnn_conv_w"].shape[1]
    nb, blk = wts["rnn_wg"].shape[1], wts["rnn_wg"].shape[2]
    pad = conv_state.shape[1]
    assert pad % SUBLANES == 0 and pad >= (kconv - 1) * step and tile >= pad and blk == LANES
    last_row = (t_real - 1) * step
    f = functools.partial(_rnn_kernel, tile=tile, pad=pad, step=step, kconv=kconv, blk=blk,
                          last_tile=last_row // tile, last_off=last_row % tile)
    grid = (g_, r_ // tile)
    in_specs = [
        _row_spec(tile, d), _row_spec(tile, d),
        pl.BlockSpec((None, pad, d), lambda g, t: (g, 0, 0)),
        pl.BlockSpec((None, step, d), lambda g, t: (g, 0, 0)),
        _layer_spec(layer, kconv, d), _layer_spec(layer, 1, d), _layer_spec(layer, nb, blk, 2 * blk),
        _layer_spec(layer, 1, d), _layer_spec(layer, 1, d), _layer_spec(layer, 1, d),
        _layer_spec(layer, d, d),
    ]
    out_specs = (_row_spec(tile, d), pl.BlockSpec((None, step, d), lambda g, t: (g, 0, 0)))
    out_shape = (jax.ShapeDtypeStruct((g_, r_, d), F32), jax.ShapeDtypeStruct((g_, step, d), F32))
    return pl.pallas_call(
        f, grid=grid, in_specs=in_specs, out_specs=out_specs, out_shape=out_shape,
        scratch_shapes=[pltpu.VMEM((pad + tile, d), F32), pltpu.VMEM((step, d), F32),
                        pltpu.VMEM((tile, d), BF16)],
        compiler_params=_params("parallel", "arbitrary"), name="rnn",
    )(x_rnn, g_rnn, conv_state, h0, wts["rnn_conv_w"], wts["rnn_conv_b"], wts["rnn_wg"], wts["rnn_ba"],
      wts["rnn_bx"], wts["rnn_lambda"], wts["w_br_rnn"])


def _prompt_attn_kernel(ql_ref, qr_ref, c_ref, kr_ref, o_ref, m_sc, l_sc, acc_sc, *, tq, tk, heads, scale):
    i = pl.program_id(1)
    rows = tq * heads
    ql = ql_ref[...]
    qr = qr_ref[...]
    m_sc[...] = jnp.full((rows, 1), -jnp.inf, F32)
    l_sc[...] = jnp.zeros((rows, 1), F32)
    acc_sc[...] = jnp.zeros(acc_sc.shape, F32)
    q_tok = i * tq + lax.broadcasted_iota(jnp.int32, (rows, 1), 0) // heads

    def step(j, masked):
        k0 = pl.multiple_of(j * tk, tk)
        c = c_ref[pl.ds(k0, tk), :]
        kr = kr_ref[pl.ds(k0, tk), :]
        s = (_dot_nt(ql, c) + _dot_nt(qr, kr)) * scale
        if masked:
            kpos = k0 + lax.broadcasted_iota(jnp.int32, (1, tk), 1)
            s = jnp.where(kpos <= q_tok, s, -jnp.inf)
        m_prev = m_sc[...]
        m_new = jnp.maximum(m_prev, jnp.max(s, axis=-1, keepdims=True))
        alpha = jnp.exp(m_prev - m_new)
        p = jnp.exp(s - m_new)
        l_sc[...] = alpha * l_sc[...] + jnp.sum(p, axis=-1, keepdims=True)
        acc_sc[...] = alpha * acc_sc[...] + _dot(p.astype(BF16), c)
        m_sc[...] = m_new

    n_full = (i * tq + 1) // tk
    n_end = ((i + 1) * tq + tk - 1) // tk

    def full_body(j, carry):
        step(j, False)
        return carry

    def masked_body(j, carry):
        step(j, True)
        return carry

    lax.fori_loop(0, n_full, full_body, 0)
    lax.fori_loop(n_full, n_end, masked_body, 0)
    o_ref[...] = (acc_sc[...] / l_sc[...]).astype(BF16)


def _prompt_attn(qlat, qrope, ckvb, krb, heads, tq, tk, scale):
    g_, r_, _ = ckvb.shape
    kv_rank = ckvb.shape[-1]
    rope = krb.shape[-1]
    ql = qlat.reshape(g_, r_ * heads, kv_rank)
    qr = qrope.reshape(g_, r_ * heads, rope)
    rows = tq * heads
    f = functools.partial(_prompt_attn_kernel, tq=tq, tk=tk, heads=heads, scale=scale)
    o = pl.pallas_call(
        f, grid=(g_, r_ // tq),
        in_specs=[_row_spec(rows, kv_rank), _row_spec(rows, rope),
                  pl.BlockSpec((None, r_, kv_rank), lambda g, t: (g, 0, 0)),
                  pl.BlockSpec((None, r_, rope), lambda g, t: (g, 0, 0))],
        out_specs=_row_spec(rows, kv_rank),
        out_shape=jax.ShapeDtypeStruct((g_, r_ * heads, kv_rank), BF16),
        scratch_shapes=[pltpu.VMEM((rows, 1), F32), pltpu.VMEM((rows, 1), F32),
                        pltpu.VMEM((rows, kv_rank), F32)],
        compiler_params=_params("parallel", "parallel"), name="prompt_attn",
    )(ql, qr, ckvb, krb)
    return o.reshape(g_, r_, heads * kv_rank)


def _sample_attn_kernel(pt_ref, ql_ref, qr_ref, cn_ref, kn_ref, cache_c, cache_r, o_ref,
                        cbuf, rbuf, sem_c, sem_r, *, layer, n_pages, page, heads, t_new, chunk, scale):
    b = pl.program_id(0)
    nb = pl.num_programs(0)

    def page_copies(seq, slot, p):
        pg = pt_ref[seq * n_pages + p]
        dst = pl.ds(p * page, page)
        return (pltpu.make_async_copy(cache_c.at[layer, pg], cbuf.at[slot, dst], sem_c.at[slot]),
                pltpu.make_async_copy(cache_r.at[layer, pg], rbuf.at[slot, dst], sem_r.at[slot]))

    def start_fetch(seq, slot):
        def body(p, carry):
            for cp in page_copies(seq, slot, p):
                cp.start()
            return carry
        lax.fori_loop(0, n_pages, body, 0)

    def wait_fetch(seq, slot):
        def body(p, carry):
            for cp in page_copies(seq, slot, p):
                cp.wait()
            return carry
        lax.fori_loop(0, n_pages, body, 0)

    slot = lax.rem(b, 2)

    @pl.when(b == 0)
    def _():
        start_fetch(b, slot)

    @pl.when(b + 1 < nb)
    def _():
        start_fetch(b + 1, 1 - slot)

    wait_fetch(b, slot)

    ql = ql_ref[...]
    qr = qr_ref[...]
    rows = ql.shape[0]
    kv_rank = ql.shape[1]

    def body(j, carry):
        m_prev, l_prev, acc = carry
        k0 = pl.multiple_of(j * chunk, chunk)
        c = cbuf[slot, pl.ds(k0, chunk), :].astype(BF16)
        kr = rbuf[slot, pl.ds(k0, chunk), :].astype(BF16)
        s = (_dot_nt(ql, c) + _dot_nt(qr, kr)) * scale
        m_new = jnp.maximum(m_prev, jnp.max(s, axis=-1, keepdims=True))
        alpha = jnp.exp(m_prev - m_new)
        p = jnp.exp(s - m_new)
        l_new = alpha * l_prev + jnp.sum(p, axis=-1, keepdims=True)
        acc = alpha * acc + _dot(p.astype(BF16), c)
        return m_new, l_new, acc

    init = (jnp.full((rows, 1), -jnp.inf, F32), jnp.zeros((rows, 1), F32), jnp.zeros((rows, kv_rank), F32))
    m_prev, l_prev, acc = lax.fori_loop(0, (n_pages * page) // chunk, body, init)

    cn = cn_ref[...].astype(BF16)
    kn = kn_ref[...].astype(BF16)
    s = (_dot_nt(ql, cn) + _dot_nt(qr, kn)) * scale
    q_tok = lax.broadcasted_iota(jnp.int32, s.shape, 0) // heads
    k_tok = lax.broadcasted_iota(jnp.int32, s.shape, 1)
    s = jnp.where((k_tok <= q_tok) & (k_tok < t_new), s, -jnp.inf)
    m_new = jnp.maximum(m_prev, jnp.max(s, axis=-1, keepdims=True))
    alpha = jnp.exp(m_prev - m_new)
    p = jnp.exp(s - m_new)
    l_new = alpha * l_prev + jnp.sum(p, axis=-1, keepdims=True)
    acc = alpha * acc + _dot(p.astype(BF16), cn)
    o_ref[...] = (acc / l_new).astype(BF16)


def _sample_attn(page_table, qlat, qrope, c_new, k_new, cache_c, cache_r, layer, heads, scale):
    bs, rows, kv_rank = qlat.shape
    rope = qrope.shape[-1]
    t_pad = c_new.shape[1]
    n_pages = page_table.shape[1]
    page = cache_c.shape[2]
    past = n_pages * page
    chunk = math.gcd(KEYS_PER_STEP, past)
    f = functools.partial(_sample_attn_kernel, layer=layer, n_pages=n_pages, page=page, heads=heads,
                          t_new=rows // heads, chunk=chunk, scale=scale)
    seq = lambda cols_rows, cols: pl.BlockSpec((None, cols_rows, cols), lambda b, pt: (b, 0, 0))
    grid_spec = pltpu.PrefetchScalarGridSpec(
        num_scalar_prefetch=1, grid=(bs,),
        in_specs=[seq(rows, kv_rank), seq(rows, rope), seq(t_pad, kv_rank), seq(t_pad, rope),
                  pl.BlockSpec(memory_space=pl.ANY), pl.BlockSpec(memory_space=pl.ANY)],
        out_specs=seq(rows, kv_rank),
        scratch_shapes=[pltpu.VMEM((2, past, kv_rank), F32), pltpu.VMEM((2, past, rope), F32),
                        pltpu.SemaphoreType.DMA((2,)), pltpu.SemaphoreType.DMA((2,))],
    )
    return pl.pallas_call(
        f, grid_spec=grid_spec, out_shape=jax.ShapeDtypeStruct((bs, rows, kv_rank), BF16),
        compiler_params=_params("arbitrary"), name="sample_attn",
    )(page_table.reshape(-1), qlat, qrope, c_new, k_new, cache_c, cache_r)


def _merge_kernel(o_ref, a_ref, ga_ref, gb_ref, x_ref, wuv_ref, wbr_ref, wout_ref, x1_ref, ov, *, heads, kv_rank, v_dim):
    for hd in range(heads):
        ov[:, hd * v_dim:(hd + 1) * v_dim] = _dot(o_ref[:, hd * kv_rank:(hd + 1) * kv_rank], wuv_ref[hd]).astype(BF16)
    out_b = _dot(ov[...], wbr_ref[...])
    merged = _sigmoid(ga_ref[...]) * a_ref[...] + _sigmoid(gb_ref[...]) * out_b
    x1_ref[...] = x_ref[...] + _dot(merged.astype(BF16), wout_ref[...])


def _merge(o_lat, out_a, g_a, g_b, x, wts, layer, tile):
    g_, r_, d = x.shape
    heads, kv_rank, v_dim = wts["w_uv"].shape[1:]
    f = functools.partial(_merge_kernel, heads=heads, kv_rank=kv_rank, v_dim=v_dim)
    return pl.pallas_call(
        f, grid=(g_, r_ // tile),
        in_specs=[_row_spec(tile, heads * kv_rank), _row_spec(tile, d), _row_spec(tile, d), _row_spec(tile, d),
                  _row_spec(tile, d), _layer_spec(layer, heads, kv_rank, v_dim),
                  _layer_spec(layer, heads * v_dim, d), _layer_spec(layer, d, d)],
        out_specs=_row_spec(tile, d), out_shape=jax.ShapeDtypeStruct((g_, r_, d), F32),
        scratch_shapes=[pltpu.VMEM((tile, heads * v_dim), BF16)],
        compiler_params=_params("parallel", "parallel"), name="merge",
    )(o_lat, out_a, g_a, g_b, x, wts["w_uv"], wts["w_br_attn"], wts["w_out"])


def _ffn_kernel(x_ref, nf_ref, st_ref, wup_ref, cw_ref, cb_ref, wdn_ref, x2_ref, ulast_ref, ubuf,
                *, tile, pad, step, kconv, ff, fc, last_tile, last_off):
    t = pl.program_id(1)

    @pl.when(t == 0)
    def _():
        ubuf[0:pad, :] = st_ref[...]

    x = x_ref[...]
    hf = _rms(x, nf_ref[...]).astype(BF16)
    acc = x
    for c in range(ff // fc):
        cs = slice(c * fc, (c + 1) * fc)
        u = _dot(hf, wup_ref[:, cs])
        v = _dot(hf, wup_ref[:, ff + c * fc:ff + (c + 1) * fc])
        ubuf[pad:pad + tile, cs] = u
        uc = cb_ref[:, cs] + cw_ref[kconv - 1:kconv, cs] * u
        for k in range(kconv - 1):
            off = pad - (kconv - 1 - k) * step
            uc = uc + cw_ref[k:k + 1, cs] * ubuf[off:off + tile, cs]
        acc = acc + _dot((_gelu(uc) * v).astype(BF16), wdn_ref[cs, :])
    x2_ref[...] = acc

    @pl.when(t == last_tile)
    def _():
        ulast_ref[...] = ubuf[pad + last_off:pad + last_off + (kconv - 1) * step, :]

    ubuf[0:pad, :] = ubuf[tile:tile + pad, :]


def _ffn(x1, conv_state, wts, layer, tile, step, t_real):
    g_, r_, d = x1.shape
    kconv, ff = wts["ffn_conv_w"].shape[1:]
    pad = conv_state.shape[1]
    keep = (kconv - 1) * step
    assert pad % SUBLANES == 0 and pad >= keep and tile >= pad
    first_row = (t_real - (kconv - 1)) * step
    assert first_row // tile == (first_row + keep - 1) // tile
    fc = 512
    f = functools.partial(_ffn_kernel, tile=tile, pad=pad, step=step, kconv=kconv, ff=ff, fc=fc,
                          last_tile=first_row // tile, last_off=first_row % tile)
    return pl.pallas_call(
        f, grid=(g_, r_ // tile),
        in_specs=[_row_spec(tile, d), _layer_spec(layer, 1, d),
                  pl.BlockSpec((None, pad, ff), lambda g, t: (g, 0, 0)),
                  _layer_spec(layer, d, 2 * ff), _layer_spec(layer, kconv, ff), _layer_spec(layer, 1, ff),
                  _layer_spec(layer, ff, d)],
        out_specs=(_row_spec(tile, d), pl.BlockSpec((None, keep, ff), lambda g, t: (g, 0, 0))),
        out_shape=(jax.ShapeDtypeStruct((g_, r_, d), F32), jax.ShapeDtypeStruct((g_, keep, ff), F32)),
        scratch_shapes=[pltpu.VMEM((pad + tile, ff), F32)],
        compiler_params=_params("parallel", "arbitrary"), name="ffn",
    )(x1, wts["norm_ffn"], conv_state, wts["ffn_w_up"], wts["ffn_conv_w"], wts["ffn_conv_b"], wts["ffn_w_down"])


def _final_norm_kernel(x_ref, g_ref, y_ref):
    y_ref[...] = _rms(x_ref[...], g_ref[...])


def _final_norm(x, g, tile):
    g_, r_, d = x.shape
    return pl.pallas_call(
        _final_norm_kernel, grid=(g_, r_ // tile),
        in_specs=[_row_spec(tile, d), pl.BlockSpec((1, d), lambda g, t: (0, 0))],
        out_specs=_row_spec(tile, d), out_shape=jax.ShapeDtypeStruct((g_, r_, d), F32),
        compiler_params=_params("parallel", "parallel"), name="final_norm",
    )(x, g)


def _rope_tables(pos, half):
    inv = ROPE_THETA ** (-jnp.arange(half, dtype=F32) / half)
    ang = pos.astype(F32)[:, None] * inv[None, :]
    c, s = jnp.cos(ang), jnp.sin(ang)
    return jnp.concatenate([c, c, c, c], axis=1), jnp.concatenate([-s, s, -s, s], axis=1)


def _swap_halves(w):
    half = w.shape[-1] // 2
    return jnp.concatenate([w[..., half:], w[..., :half]], axis=-1)


def _pack_weights(p, d, q_rank, kv_rank, heads, nope, rope):
    depth = p["w_in"].shape[0]
    w_in = p["w_in"]
    o_cq = 2 * d
    o_kr = o_cq + q_rank + kv_rank
    o_ga = o_kr + rope
    k_r = w_in[:, :, o_kr:o_ga]
    w_in_p = jnp.concatenate([w_in[:, :, :2 * d], w_in[:, :, o_ga:o_ga + 2 * d], w_in[:, :, o_cq:o_kr],
                              k_r, _swap_halves(k_r)], axis=-1).astype(BF16)
    w_uq = p["w_uq"].reshape(depth, q_rank, heads, nope + rope)
    q_r = w_uq[..., nope:]
    w_uq_p = jnp.concatenate([w_uq[..., :nope].reshape(depth, q_rank, heads * nope),
                              q_r.reshape(depth, q_rank, heads * rope),
                              _swap_halves(q_r).reshape(depth, q_rank, heads * rope)], axis=-1).astype(BF16)
    row = lambda a: a[:, None, :]
    return {
        "norm_mix": row(p["norm_mix"]), "w_in": w_in_p, "q_norm": row(p["q_norm"]), "kv_norm": row(p["kv_norm"]),
        "w_uq": w_uq_p, "w_uk": p["w_uk"].transpose(0, 2, 3, 1).astype(BF16),
        "w_uv": p["w_uv"].transpose(0, 2, 1, 3).astype(BF16),
        "rnn_conv_w": p["rnn_conv_w"], "rnn_conv_b": row(p["rnn_conv_b"]),
        "rnn_wg": jnp.concatenate([p["rnn_wa"], p["rnn_wx"]], axis=-1).astype(BF16),
        "rnn_ba": row(p["rnn_ba"]), "rnn_bx": row(p["rnn_bx"]), "rnn_lambda": row(p["rnn_lambda"]),
        "w_br_rnn": p["w_br_rnn"].astype(BF16), "w_br_attn": p["w_br_attn"].astype(BF16),
        "w_out": p["w_out"].astype(BF16), "norm_ffn": row(p["norm_ffn"]),
        "ffn_w_up": p["ffn_w_up"].astype(BF16), "ffn_conv_w": p["ffn_conv_w"],
        "ffn_conv_b": row(p["ffn_conv_b"]), "ffn_w_down": p["ffn_w_down"].astype(BF16),
    }


def _pad_rows(a, rows):
    return jnp.pad(a, ((0, 0), (rows - a.shape[1], 0), (0, 0)))


def kernel(x_prompt, x_sample, cache_kv_latent, cache_k_rope, state_rnn_h, state_rnn_conv, state_ffn_conv,
           page_table, meta_tokens, norm_mix, w_in, rnn_conv_w, rnn_conv_b, rnn_wa, rnn_ba, rnn_wx, rnn_bx,
           rnn_lambda, q_norm, w_uq, kv_norm, w_uk, w_uv, w_br_rnn, w_br_attn, w_out, norm_ffn, ffn_w_up,
           ffn_conv_w, ffn_conv_b, ffn_w_down, norm_final):
    bp, seq, d = x_prompt.shape
    bs, ts, _ = x_sample.shape
    depth = w_in.shape[0]
    n_meta = meta_tokens.shape[0]
    kv_rank, heads, nope = w_uk.shape[1:]
    q_rank = w_uq.shape[1]
    rope = w_uq.shape[2] // heads - nope
    ff = ffn_w_down.shape[1]
    rnn_k = rnn_conv_w.shape[1]
    ffn_k = ffn_conv_w.shape[1]
    past_len = page_table.shape[1] * cache_kv_latent.shape[2]
    scale = 1.0 / math.sqrt(nope + rope)
    assert 2 * rope == LANES and kv_rank % LANES == 0 and nope % LANES == 0
    dims = (q_rank, kv_rank, heads, nope, rope)

    wts = _pack_weights(
        dict(norm_mix=norm_mix, w_in=w_in, rnn_conv_w=rnn_conv_w, rnn_conv_b=rnn_conv_b, rnn_wa=rnn_wa,
             rnn_ba=rnn_ba, rnn_wx=rnn_wx, rnn_bx=rnn_bx, rnn_lambda=rnn_lambda, q_norm=q_norm, w_uq=w_uq,
             kv_norm=kv_norm, w_uk=w_uk, w_uv=w_uv, w_br_rnn=w_br_rnn, w_br_attn=w_br_attn, w_out=w_out,
             norm_ffn=norm_ffn, ffn_w_up=ffn_w_up, ffn_conv_w=ffn_conv_w, ffn_conv_b=ffn_conv_b,
             ffn_w_down=ffn_w_down),
        d, q_rank, kv_rank, heads, nope, rope)

    lp = seq + n_meta
    lpad = _round_up(lp, PROMPT_TILE)
    xp = jnp.concatenate([jnp.broadcast_to(meta_tokens[None].astype(x_prompt.dtype), (bp, n_meta, d)), x_prompt,
                          jnp.zeros((bp, lpad - lp, d), x_prompt.dtype)], axis=1)
    cos_p, sin_p = _rope_tables(jnp.arange(lpad, dtype=jnp.int32), rope // 2)
    p_pad = SUBLANES
    zeros_p = lambda c: jnp.zeros((bp, p_pad, c), F32)
    rs = ts * bs
    xs = x_sample.transpose(1, 0, 2).reshape(1, rs, d)
    cos_s, sin_s = _rope_tables(past_len + jnp.repeat(jnp.arange(ts, dtype=jnp.int32), bs), rope // 2)
    t_pad = _round_up(ts, 2 * SUBLANES)

    def time_major(state):
        return state.transpose(1, 0, 2).reshape(1, state.shape[1] * bs, state.shape[2])

    def batch_major(a, k):
        return a.reshape(k, bs, a.shape[-1]).transpose(1, 0, 2)

    outs = {k: [] for k in ("p_lat", "p_rope", "p_h", "p_rc", "p_fc", "s_lat", "s_rope", "s_h", "s_rc", "s_fc")}
    for l in range(depth):
        x_rnn, g_rnn, g_a, g_b, ckv, ckvb, kr, krb, qlat, qrope = _in_proj(xp, cos_p, sin_p, wts, l, PROMPT_TILE, dims)
        out_a, h_last = _rnn(x_rnn, g_rnn, zeros_p(d), jnp.zeros((bp, 1, d), F32), wts, l, PROMPT_TILE, 1, lp)
        o_lat = _prompt_attn(qlat, qrope, ckvb, krb, heads, PROMPT_TILE, PROMPT_TILE, scale)
        x1 = _merge(o_lat, out_a, g_a, g_b, xp, wts, l, PROMPT_TILE)
        xp, u_last = _ffn(x1, zeros_p(ff), wts, l, PROMPT_TILE, 1, lp)
        outs["p_lat"].append(ckv[:, :lp])
        outs["p_rope"].append(kr[:, :lp])
        outs["p_h"].append(h_last[:, 0])
        outs["p_rc"].append(x_rnn[:, lp - (rnn_k - 1):lp])
        outs["p_fc"].append(u_last)

        x_rnn, g_rnn, g_a, g_b, ckv, ckvb, kr, krb, qlat, qrope = _in_proj(xs, cos_s, sin_s, wts, l, rs, dims)
        rc0 = time_major(state_rnn_conv[l])
        out_a, h_last = _rnn(x_rnn, g_rnn, _pad_rows(rc0, _round_up(rc0.shape[1], SUBLANES)),
                             state_rnn_h[l][None], wts, l, rs, bs, ts)
        to_seq = lambda a: jnp.pad(batch_major(a, ts), ((0, 0), (0, t_pad - ts), (0, 0)))
        o_seq = _sample_attn(page_table, batch_major(qlat, ts).reshape(bs, ts * heads, kv_rank),
                             batch_major(qrope, ts).reshape(bs, ts * heads, rope),
                             to_seq(ckv), to_seq(kr), cache_kv_latent, cache_k_rope, l, heads, scale)
        o_lat = o_seq.reshape(bs, ts, heads * kv_rank).transpose(1, 0, 2).reshape(1, rs, heads * kv_rank)
        x1 = _merge(o_lat, out_a, g_a, g_b, xs, wts, l, rs)
        fc0 = time_major(state_ffn_conv[l])
        xs, u_last = _ffn(x1, _pad_rows(fc0, _round_up(fc0.shape[1], SUBLANES)), wts, l, rs, bs, ts)
        outs["s_lat"].append(batch_major(ckv, ts))
        outs["s_rope"].append(batch_major(kr, ts))
        outs["s_h"].append(h_last[0])
        outs["s_rc"].append(batch_major(x_rnn[:, (ts - (rnn_k - 1)) * bs:], rnn_k - 1))
        outs["s_fc"].append(batch_major(u_last, ffn_k - 1))

    y_prompt = _final_norm(xp, norm_final[None], PROMPT_TILE)[:, n_meta:lp]
    y_sample = batch_major(_final_norm(xs, norm_final[None], rs), ts)
    st = {k: jnp.stack(v) for k, v in outs.items()}
    return (y_prompt, y_sample, st["p_lat"], st["p_rope"], st["p_h"], st["p_rc"], st["p_fc"],
            st["s_lat"], st["s_rope"], st["s_h"], st["s_rc"], st["s_fc"])
```

```python
import functools
import math

import jax
import jax.numpy as jnp
from jax import lax
from jax.experimental import pallas as pl
from jax.experimental.pallas import tpu as pltpu

F32 = jnp.float32
BF16 = jnp.bfloat16

EPS = 1e-6
RGLRU_C = 8.0
ROPE_THETA = 10000.0
LANES = 128
SUBLANES = 8
VMEM_LIMIT = 56 * 1024 * 1024
LOG2E = 1.4426950408889634
PROMPT_TILE = 256
ATTN_KEY_TILE = 512
HEADS_AHEAD = 2
PAGES_PER_CHUNK = 8
CHUNKS_AHEAD = 2
DMA_UNROLL = 8


def _round_up(x, m):
    return (x + m - 1) // m * m


def _rms(x, g):
    return x * lax.rsqrt(jnp.mean(x * x, axis=-1, keepdims=True) + EPS) * g


def _gelu(x):
    return 0.5 * x * (1.0 + jnp.tanh(math.sqrt(2.0 / math.pi) * (x + 0.044715 * (x * x * x))))


def _sigmoid(x):
    return 1.0 / (1.0 + jnp.exp(-x))


def _dot(a, b):
    return jnp.dot(a, b, preferred_element_type=F32)


def _dot_nt(a, b):
    return lax.dot_general(a, b, (((1,), (1,)), ((), ())), preferred_element_type=F32)


def _params(*sem):
    return pltpu.CompilerParams(dimension_semantics=sem, vmem_limit_bytes=VMEM_LIMIT)


def _row_spec(tile, cols):
    return pl.BlockSpec((None, tile, cols), lambda g, t: (g, t, 0))


def _layer_spec(layer, *shape):
    zeros = (0,) * len(shape)
    return pl.BlockSpec((None,) + tuple(shape), lambda g, t: (layer,) + zeros)


def _in_proj_kernel(x_ref, nm_ref, w_ref, qn_ref, kvn_ref, wuq_ref, wuk_ref, cos_ref, sin_ref,
                    xrnn_ref, grnn_ref, ga_ref, gb_ref, ckv_ref, ckvb_ref, kr_ref, krb_ref,
                    qlat_ref, qrope_ref, *, d, q_rank, kv_rank, heads, nope, rope):
    h = _rms(x_ref[...], nm_ref[...]).astype(BF16)

    def proj(c0, c1):
        return _dot(h, w_ref[:, c0:c1])

    xrnn_ref[...] = proj(0, d)
    grnn_ref[...] = proj(d, 2 * d)
    ga_ref[...] = proj(2 * d, 3 * d)
    gb_ref[...] = proj(3 * d, 4 * d)
    o = 4 * d
    c_q = proj(o, o + q_rank)
    o += q_rank
    c_kv = _rms(proj(o, o + kv_rank), kvn_ref[...])
    o += kv_rank
    ckv_ref[...] = c_kv
    ckvb_ref[...] = c_kv.astype(BF16)
    cos = cos_ref[...]
    sin = sin_ref[...]
    kk = proj(o, o + 2 * rope)
    k_rot = (kk * cos + pltpu.roll(kk, rope, 1) * sin)[:, :rope]
    kr_ref[...] = k_rot
    krb_ref[...] = k_rot.astype(BF16)

    cqn = _rms(c_q, qn_ref[...]).astype(BF16)
    hn = heads * nope
    hr = heads * rope
    for j in range(hr // LANES):
        qr = _dot(cqn, wuq_ref[:, hn + j * LANES:hn + (j + 1) * LANES])
        qp = _dot(cqn, wuq_ref[:, hn + hr + j * LANES:hn + hr + (j + 1) * LANES])
        qrope_ref[:, j * LANES:(j + 1) * LANES] = (qr * cos + qp * sin).astype(BF16)
    for hd in range(heads):
        qn = _dot(cqn, wuq_ref[:, hd * nope:(hd + 1) * nope]).astype(BF16)
        qlat_ref[:, hd * kv_rank:(hd + 1) * kv_rank] = _dot(qn, wuk_ref[hd]).astype(BF16)


def _in_proj(x, cos4, sin4, wts, layer, tile, dims):
    g_, r_, d = x.shape
    q_rank, kv_rank, heads, nope, rope = dims
    width = wts["w_in"].shape[-1]
    grid = (g_, r_ // tile)
    f = functools.partial(_in_proj_kernel, d=d, q_rank=q_rank, kv_rank=kv_rank, heads=heads,
                          nope=nope, rope=rope)
    tab = pl.BlockSpec((tile, LANES), lambda g, t: (t, 0))
    out_shape = (
        jax.ShapeDtypeStruct((g_, r_, d), F32), jax.ShapeDtypeStruct((g_, r_, d), F32),
        jax.ShapeDtypeStruct((g_, r_, d), F32), jax.ShapeDtypeStruct((g_, r_, d), F32),
        jax.ShapeDtypeStruct((g_, r_, kv_rank), F32), jax.ShapeDtypeStruct((g_, r_, kv_rank), BF16),
        jax.ShapeDtypeStruct((g_, r_, rope), F32), jax.ShapeDtypeStruct((g_, r_, rope), BF16),
        jax.ShapeDtypeStruct((g_, r_, heads * kv_rank), BF16),
        jax.ShapeDtypeStruct((g_, r_, heads * rope), BF16),
    )
    out_specs = (
        _row_spec(tile, d), _row_spec(tile, d), _row_spec(tile, d), _row_spec(tile, d),
        _row_spec(tile, kv_rank), _row_spec(tile, kv_rank), _row_spec(tile, rope), _row_spec(tile, rope),
        _row_spec(tile, heads * kv_rank), _row_spec(tile, heads * rope),
    )
    in_specs = [
        _row_spec(tile, d), _layer_spec(layer, 1, d), _layer_spec(layer, d, width),
        _layer_spec(layer, 1, q_rank), _layer_spec(layer, 1, kv_rank),
        _layer_spec(layer, q_rank, wts["w_uq"].shape[-1]), _layer_spec(layer, heads, nope, kv_rank),
        tab, tab,
    ]
    return pl.pallas_call(
        f, grid=grid, in_specs=in_specs, out_specs=out_specs, out_shape=out_shape,
        compiler_params=_params("parallel", "parallel"), name="in_proj",
    )(x, wts["norm_mix"], wts["w_in"], wts["q_norm"], wts["kv_norm"], wts["w_uq"], wts["w_uk"], cos4, sin4)


def _scan_rows(a, u, step):
    n = a.shape[0]
    row = lax.broadcasted_iota(jnp.int32, a.shape, 0)
    d = step
    while d < n:
        keep = row >= d
        a_prev = jnp.where(keep, pltpu.roll(a, d, 0), 1.0)
        u_prev = jnp.where(keep, pltpu.roll(u, d, 0), 0.0)
        u = a * u_prev + u
        a = a * a_prev
        d *= 2
    return a, u


def _rnn_kernel(x_ref, g_ref, st_ref, h0_ref, cw_ref, cb_ref, wg_ref, ba_ref, bx_ref, lam_ref, wbr_ref,
                out_ref, hlast_ref, xbuf, hcar, gated, *, tile, pad, step, kconv, blk, last_tile, last_off):
    t = pl.program_id(1)

    @pl.when(t == 0)
    def _():
        xbuf[0:pad, :] = st_ref[...]
        hcar[...] = h0_ref[...]

    xbuf[pad:pad + tile, :] = x_ref[...]
    d = x_ref.shape[-1]
    for n in range(d // blk):
        cs = slice(n * blk, (n + 1) * blk)
        xc = cb_ref[:, cs] + cw_ref[kconv - 1:kconv, cs] * xbuf[pad:pad + tile, cs]
        for k in range(kconv - 1):
            off = pad - (kconv - 1 - k) * step
            xc = xc + cw_ref[k:k + 1, cs] * xbuf[off:off + tile, cs]
        gates = _dot(xc.astype(BF16), wg_ref[n])
        r = _sigmoid(gates[:, :blk] + ba_ref[:, cs])
        i = _sigmoid(gates[:, blk:] + bx_ref[:, cs])
        neg_lam = -lam_ref[:, cs]
        softplus = jnp.maximum(neg_lam, 0.0) + jnp.log(1.0 + jnp.exp(-jnp.abs(neg_lam)))
        log_a = -RGLRU_C * r * softplus
        a = jnp.exp(log_a)
        u = jnp.sqrt(1.0 - jnp.exp(2.0 * log_a)) * (i * xc)
        a_p, u_p = _scan_rows(a, u, step)
        h_in = hcar[:, cs]
        if step > 1:
            h_in = jnp.concatenate([h_in] * (tile // step), axis=0)
        hs = a_p * h_in + u_p
        hcar[:, cs] = hs[tile - step:tile, :]

        @pl.when(t == last_tile)
        def _():
            hlast_ref[:, cs] = hs[last_off:last_off + step, :]

        gated[:, cs] = (hs * _gelu(g_ref[:, cs])).astype(BF16)
    out_ref[...] = _dot(gated[...], wbr_ref[...])
    xbuf[0:pad, :] = xbuf[tile:tile + pad, :]


def _rnn(x_rnn, g_rnn, conv_state, h0, wts, layer, tile, step, t_real):
    g_, r_, d = x_rnn.shape
    kconv = wts["rnn_conv_w"].shape[1]
    nb, blk = wts["rnn_wg"].shape[1], wts["rnn_wg"].shape[2]
    pad = conv_state.shape[1]
    assert pad % SUBLANES == 0 and pad >= (kconv - 1) * step and tile >= pad and blk == LANES
    last_row = (t_real - 1) * step
    f = functools.partial(_rnn_kernel, tile=tile, pad=pad, step=step, kconv=kconv, blk=blk,
                          last_tile=last_row // tile, last_off=last_row % tile)
    grid = (g_, r_ // tile)
    in_specs = [
        _row_spec(tile, d), _row_spec(tile, d),
        pl.BlockSpec((None, pad, d), lambda g, t: (g, 0, 0)),
        pl.BlockSpec((None, step, d), lambda g, t: (g, 0, 0)),
        _layer_spec(layer, kconv, d), _layer_spec(layer, 1, d), _layer_spec(layer, nb, blk, 2 * blk),
        _layer_spec(layer, 1, d), _layer_spec(layer, 1, d), _layer_spec(layer, 1, d),
        _layer_spec(layer, d, d),
    ]
    out_specs = (_row_spec(tile, d), pl.BlockSpec((None, step, d), lambda g, t: (g, 0, 0)))
    out_shape = (jax.ShapeDtypeStruct((g_, r_, d), F32), jax.ShapeDtypeStruct((g_, step, d), F32))
    return pl.pallas_call(
        f, grid=grid, in_specs=in_specs, out_specs=out_specs, out_shape=out_shape,
        scratch_shapes=[pltpu.VMEM((pad + tile, d), F32), pltpu.VMEM((step, d), F32),
                        pltpu.VMEM((tile, d), BF16)],
        compiler_params=_params("parallel", "arbitrary"), name="rnn",
    )(x_rnn, g_rnn, conv_state, h0, wts["rnn_conv_w"], wts["rnn_conv_b"], wts["rnn_wg"], wts["rnn_ba"],
      wts["rnn_bx"], wts["rnn_lambda"], wts["w_br_rnn"])


def _prompt_attn_kernel(ql_ref, qr_ref, c_ref, kr_ref, o_ref, m_sc, l_sc, acc_sc, *, tq, tk, heads, kv, rope, coef):
    i = pl.program_id(1)
    m_sc[...] = jnp.full(m_sc.shape, -jnp.inf, F32)
    l_sc[...] = jnp.zeros(l_sc.shape, F32)
    acc_sc[...] = jnp.zeros(acc_sc.shape, F32)

    def step(k0, width, masked):
        c = c_ref[pl.ds(k0, width), :]
        kr = kr_ref[pl.ds(k0, width), :]
        if masked:
            visible = (lax.broadcasted_iota(jnp.int32, (tq, width), 1)
                       <= lax.broadcasted_iota(jnp.int32, (tq, width), 0))

        def scores(h):
            return _dot_nt(ql_ref[:, h * kv:(h + 1) * kv], c) + _dot_nt(qr_ref[:, h * rope:(h + 1) * rope], kr)

        pend = [scores(h) for h in range(min(HEADS_AHEAD, heads))]
        for h in range(heads):
            s = pend.pop(0)
            if h + HEADS_AHEAD < heads:
                pend.append(scores(h + HEADS_AHEAD))
            if masked:
                s = jnp.where(visible, s, -jnp.inf)
            m_prev = m_sc[h]
            m_new = jnp.maximum(m_prev, jnp.max(s, axis=1, keepdims=True))
            alpha = jnp.exp2((m_prev - m_new) * coef)
            p = jnp.exp2((s - jnp.concatenate([m_new] * (width // LANES), axis=1)) * coef)
            psum = p[:, :LANES]
            for t in range(1, width // LANES):
                psum = psum + p[:, t * LANES:(t + 1) * LANES]
            l_sc[h] = alpha * l_sc[h] + psum
            acc_sc[h] = jnp.concatenate([alpha] * (kv // LANES), axis=1) * acc_sc[h] + _dot(p.astype(BF16), c)
            m_sc[h] = m_new

    ratio = tk // tq
    n_wide = i // ratio

    def wide_body(j, carry):
        step(pl.multiple_of(j * tk, tk), tk, False)
        return carry

    def narrow_body(j, carry):
        step(pl.multiple_of(j * tq, tq), tq, False)
        return carry

    lax.fori_loop(0, n_wide, wide_body, 0)
    lax.fori_loop(n_wide * ratio, i, narrow_body, 0)
    step(pl.multiple_of(i * tq, tq), tq, True)
    for h in range(heads):
        l = jnp.sum(l_sc[h], axis=1, keepdims=True)
        o_ref[:, h * kv:(h + 1) * kv] = (acc_sc[h] / l).astype(BF16)


def _prompt_attn(qlat, qrope, ckvb, krb, heads, tq, tk, scale):
    g_, r_, kv_rank = ckvb.shape
    rope = krb.shape[-1]
    assert tk % tq == 0 and r_ % tq == 0 and tq % LANES == 0
    f = functools.partial(_prompt_attn_kernel, tq=tq, tk=tk, heads=heads, kv=kv_rank, rope=rope,
                          coef=scale * LOG2E)
    return pl.pallas_call(
        f, grid=(g_, r_ // tq),
        in_specs=[_row_spec(tq, heads * kv_rank), _row_spec(tq, heads * rope),
                  pl.BlockSpec((None, r_, kv_rank), lambda g, t: (g, 0, 0)),
                  pl.BlockSpec((None, r_, rope), lambda g, t: (g, 0, 0))],
        out_specs=_row_spec(tq, heads * kv_rank),
        out_shape=jax.ShapeDtypeStruct((g_, r_, heads * kv_rank), BF16),
        scratch_shapes=[pltpu.VMEM((heads, tq, LANES), F32), pltpu.VMEM((heads, tq, LANES), F32),
                        pltpu.VMEM((heads, tq, kv_rank), F32)],
        compiler_params=_params("parallel", "parallel"), name="prompt_attn",
    )(qlat, qrope, ckvb, krb)


def _sample_attn_kernel(pt_ref, ql_ref, qr_ref, cn_ref, kn_ref, cache_c, cache_rt, o_ref,
                        cbuf, rbuf, sem_c, sem_r, *, layer, n_pages, page, heads, t_new, cpages, coef):
    b = pl.program_id(0)
    nb = pl.num_programs(0)

    def page_copies(seq, slot, p):
        pg = pt_ref[seq * n_pages + p]
        return (pltpu.make_async_copy(cache_c.at[layer, pg], cbuf.at[slot, p], sem_c.at[slot]),
                pltpu.make_async_copy(cache_rt.at[layer, pg], rbuf.at[slot, p], sem_r.at[slot]))

    def start_fetch(seq, slot):
        def body(p, carry):
            for cp in page_copies(seq, slot, p):
                cp.start()
            return carry
        lax.fori_loop(0, n_pages, body, 0, unroll=DMA_UNROLL)

    def wait_fetch(seq, slot):
        def body(p, carry):
            for cp in page_copies(seq, slot, p):
                cp.wait()
            return carry
        lax.fori_loop(0, n_pages, body, 0, unroll=DMA_UNROLL)

    slot = lax.rem(b, 2)

    @pl.when(b == 0)
    def _():
        start_fetch(b, slot)

    @pl.when(b + 1 < nb)
    def _():
        start_fetch(b + 1, 1 - slot)

    wait_fetch(b, slot)

    ql = ql_ref[...]
    qr = qr_ref[...]
    kv = ql.shape[1]
    ck = cpages * page
    n_chunks = n_pages // cpages

    def scores(j):
        c = cbuf[slot, j * cpages:(j + 1) * cpages].reshape(ck, kv).astype(BF16)
        s_rope = jnp.concatenate([_dot(qr, rbuf[slot, j * cpages + p].astype(BF16)) for p in range(cpages)], axis=1)
        return _dot_nt(ql, c) + s_rope, c

    cn = cn_ref[...].astype(BF16)
    kn = kn_ref[...].astype(BF16)
    s_new = _dot_nt(ql, cn) + _dot_nt(qr, kn)
    q_tok = lax.broadcasted_iota(jnp.int32, s_new.shape, 0) // heads
    k_tok = lax.broadcasted_iota(jnp.int32, s_new.shape, 1)
    s_new = jnp.where((k_tok <= q_tok) & (k_tok < t_new), s_new, -jnp.inf)
    m = jnp.max(s_new, axis=1, keepdims=True)
    p_new = jnp.exp2((s_new - m) * coef)
    l = jnp.sum(p_new, axis=1, keepdims=True)
    acc = _dot(p_new.astype(BF16), cn)

    pend = [scores(j) for j in range(min(CHUNKS_AHEAD, n_chunks))]
    for j in range(n_chunks):
        s, c = pend.pop(0)
        if j + CHUNKS_AHEAD < n_chunks:
            pend.append(scores(j + CHUNKS_AHEAD))
        m_new = jnp.maximum(m, jnp.max(s, axis=1, keepdims=True))
        alpha = jnp.exp2((m - m_new) * coef)
        p = jnp.exp2((s - m_new) * coef)
        l = alpha * l + jnp.sum(p, axis=1, keepdims=True)
        acc = alpha * acc + _dot(p.astype(BF16), c)
        m = m_new
    o_ref[...] = (acc / l).astype(BF16)


def _sample_attn(page_table, qlat, qrope, c_new, k_new, cache_c, cache_rt, layer, heads, scale):
    bs, rows, kv_rank = qlat.shape
    rope = qrope.shape[-1]
    t_pad = c_new.shape[1]
    n_pages = page_table.shape[1]
    page = cache_c.shape[2]
    cpages = math.gcd(PAGES_PER_CHUNK, n_pages)
    f = functools.partial(_sample_attn_kernel, layer=layer, n_pages=n_pages, page=page, heads=heads,
                          t_new=rows // heads, cpages=cpages, coef=scale * LOG2E)
    seq = lambda r, c: pl.BlockSpec((None, r, c), lambda b, pt: (b, 0, 0))
    grid_spec = pltpu.PrefetchScalarGridSpec(
        num_scalar_prefetch=1, grid=(bs,),
        in_specs=[seq(rows, kv_rank), seq(rows, rope), seq(t_pad, kv_rank), seq(t_pad, rope),
                  pl.BlockSpec(memory_space=pl.ANY), pl.BlockSpec(memory_space=pl.ANY)],
        out_specs=seq(rows, kv_rank),
        scratch_shapes=[pltpu.VMEM((2, n_pages, page, kv_rank), F32), pltpu.VMEM((2, n_pages, rope, page), F32),
                        pltpu.SemaphoreType.DMA((2,)), pltpu.SemaphoreType.DMA((2,))],
    )
    return pl.pallas_call(
        f, grid_spec=grid_spec, out_shape=jax.ShapeDtypeStruct((bs, rows, kv_rank), BF16),
        compiler_params=_params("arbitrary"), name="sample_attn",
    )(page_table.reshape(-1), qlat, qrope, c_new, k_new, cache_c, cache_rt)


def _merge_kernel(o_ref, a_ref, ga_ref, gb_ref, x_ref, wuv_ref, wbr_ref, wout_ref, x1_ref, ov, *, heads, kv_rank, v_dim):
    for hd in range(heads):
        ov[:, hd * v_dim:(hd + 1) * v_dim] = _dot(o_ref[:, hd * kv_rank:(hd + 1) * kv_rank], wuv_ref[hd]).astype(BF16)
    out_b = _dot(ov[...], wbr_ref[...])
    merged = _sigmoid(ga_ref[...]) * a_ref[...] + _sigmoid(gb_ref[...]) * out_b
    x1_ref[...] = x_ref[...] + _dot(merged.astype(BF16), wout_ref[...])


def _merge(o_lat, out_a, g_a, g_b, x, wts, layer, tile):
    g_, r_, d = x.shape
    heads, kv_rank, v_dim = wts["w_uv"].shape[1:]
    f = functools.partial(_merge_kernel, heads=heads, kv_rank=kv_rank, v_dim=v_dim)
    return pl.pallas_call(
        f, grid=(g_, r_ // tile),
        in_specs=[_row_spec(tile, heads * kv_rank), _row_spec(tile, d), _row_spec(tile, d), _row_spec(tile, d),
                  _row_spec(tile, d), _layer_spec(layer, heads, kv_rank, v_dim),
                  _layer_spec(layer, heads * v_dim, d), _layer_spec(layer, d, d)],
        out_specs=_row_spec(tile, d), out_shape=jax.ShapeDtypeStruct((g_, r_, d), F32),
        scratch_shapes=[pltpu.VMEM((tile, heads * v_dim), BF16)],
        compiler_params=_params("parallel", "parallel"), name="merge",
    )(o_lat, out_a, g_a, g_b, x, wts["w_uv"], wts["w_br_attn"], wts["w_out"])


def _ffn_kernel(x_ref, nf_ref, st_ref, wup_ref, cw_ref, cb_ref, wdn_ref, x2_ref, ulast_ref, ubuf,
                *, tile, pad, step, kconv, ff, fc, last_tile, last_off):
    t = pl.program_id(1)

    @pl.when(t == 0)
    def _():
        ubuf[0:pad, :] = st_ref[...]

    x = x_ref[...]
    hf = _rms(x, nf_ref[...]).astype(BF16)
    acc = x
    for c in range(ff // fc):
        cs = slice(c * fc, (c + 1) * fc)
        u = _dot(hf, wup_ref[:, cs])
        v = _dot(hf, wup_ref[:, ff + c * fc:ff + (c + 1) * fc])
        ubuf[pad:pad + tile, cs] = u
        uc = cb_ref[:, cs] + cw_ref[kconv - 1:kconv, cs] * u
        for k in range(kconv - 1):
            off = pad - (kconv - 1 - k) * step
            uc = uc + cw_ref[k:k + 1, cs] * ubuf[off:off + tile, cs]
        acc = acc + _dot((_gelu(uc) * v).astype(BF16), wdn_ref[cs, :])
    x2_ref[...] = acc

    @pl.when(t == last_tile)
    def _():
        ulast_ref[...] = ubuf[pad + last_off:pad + last_off + (kconv - 1) * step, :]

    ubuf[0:pad, :] = ubuf[tile:tile + pad, :]


def _ffn(x1, conv_state, wts, layer, tile, step, t_real):
    g_, r_, d = x1.shape
    kconv, ff = wts["ffn_conv_w"].shape[1:]
    pad = conv_state.shape[1]
    keep = (kconv - 1) * step
    assert pad % SUBLANES == 0 and pad >= keep and tile >= pad
    first_row = (t_real - (kconv - 1)) * step
    assert first_row // tile == (first_row + keep - 1) // tile
    fc = 512
    f = functools.partial(_ffn_kernel, tile=tile, pad=pad, step=step, kconv=kconv, ff=ff, fc=fc,
                          last_tile=first_row // tile, last_off=first_row % tile)
    return pl.pallas_call(
        f, grid=(g_, r_ // tile),
        in_specs=[_row_spec(tile, d), _layer_spec(layer, 1, d),
                  pl.BlockSpec((None, pad, ff), lambda g, t: (g, 0, 0)),
                  _layer_spec(layer, d, 2 * ff), _layer_spec(layer, kconv, ff), _layer_spec(layer, 1, ff),
                  _layer_spec(layer, ff, d)],
        out_specs=(_row_spec(tile, d), pl.BlockSpec((None, keep, ff), lambda g, t: (g, 0, 0))),
        out_shape=(jax.ShapeDtypeStruct((g_, r_, d), F32), jax.ShapeDtypeStruct((g_, keep, ff), F32)),
        scratch_shapes=[pltpu.VMEM((pad + tile, ff), F32)],
        compiler_params=_params("parallel", "arbitrary"), name="ffn",
    )(x1, wts["norm_ffn"], conv_state, wts["ffn_w_up"], wts["ffn_conv_w"], wts["ffn_conv_b"], wts["ffn_w_down"])


def _final_norm_kernel(x_ref, g_ref, y_ref):
    y_ref[...] = _rms(x_ref[...], g_ref[...])


def _final_norm(x, g, tile):
    g_, r_, d = x.shape
    return pl.pallas_call(
        _final_norm_kernel, grid=(g_, r_ // tile),
        in_specs=[_row_spec(tile, d), pl.BlockSpec((1, d), lambda g, t: (0, 0))],
        out_specs=_row_spec(tile, d), out_shape=jax.ShapeDtypeStruct((g_, r_, d), F32),
        compiler_params=_params("parallel", "parallel"), name="final_norm",
    )(x, g)


def _rope_tables(pos, half):
    inv = ROPE_THETA ** (-jnp.arange(half, dtype=F32) / half)
    ang = pos.astype(F32)[:, None] * inv[None, :]
    c, s = jnp.cos(ang), jnp.sin(ang)
    return jnp.concatenate([c, c, c, c], axis=1), jnp.concatenate([-s, s, -s, s], axis=1)


def _swap_halves(w):
    half = w.shape[-1] // 2
    return jnp.concatenate([w[..., half:], w[..., :half]], axis=-1)


def _pack_weights(p, d, q_rank, kv_rank, heads, nope, rope):
    depth = p["w_in"].shape[0]
    w_in = p["w_in"]
    o_cq = 2 * d
    o_kr = o_cq + q_rank + kv_rank
    o_ga = o_kr + rope
    k_r = w_in[:, :, o_kr:o_ga]
    w_in_p = jnp.concatenate([w_in[:, :, :2 * d], w_in[:, :, o_ga:o_ga + 2 * d], w_in[:, :, o_cq:o_kr],
                              k_r, _swap_halves(k_r)], axis=-1).astype(BF16)
    w_uq = p["w_uq"].reshape(depth, q_rank, heads, nope + rope)
    q_r = w_uq[..., nope:]
    w_uq_p = jnp.concatenate([w_uq[..., :nope].reshape(depth, q_rank, heads * nope),
                              q_r.reshape(depth, q_rank, heads * rope),
                              _swap_halves(q_r).reshape(depth, q_rank, heads * rope)], axis=-1).astype(BF16)
    row = lambda a: a[:, None, :]
    return {
        "norm_mix": row(p["norm_mix"]), "w_in": w_in_p, "q_norm": row(p["q_norm"]), "kv_norm": row(p["kv_norm"]),
        "w_uq": w_uq_p, "w_uk": p["w_uk"].transpose(0, 2, 3, 1).astype(BF16),
        "w_uv": p["w_uv"].transpose(0, 2, 1, 3).astype(BF16),
        "rnn_conv_w": p["rnn_conv_w"], "rnn_conv_b": row(p["rnn_conv_b"]),
        "rnn_wg": jnp.concatenate([p["rnn_wa"], p["rnn_wx"]], axis=-1).astype(BF16),
        "rnn_ba": row(p["rnn_ba"]), "rnn_bx": row(p["rnn_bx"]), "rnn_lambda": row(p["rnn_lambda"]),
        "w_br_rnn": p["w_br_rnn"].astype(BF16), "w_br_attn": p["w_br_attn"].astype(BF16),
        "w_out": p["w_out"].astype(BF16), "norm_ffn": row(p["norm_ffn"]),
        "ffn_w_up": p["ffn_w_up"].astype(BF16), "ffn_conv_w": p["ffn_conv_w"],
        "ffn_conv_b": row(p["ffn_conv_b"]), "ffn_w_down": p["ffn_w_down"].astype(BF16),
    }


def _pad_rows(a, rows):
    return jnp.pad(a, ((0, 0), (rows - a.shape[1], 0), (0, 0)))


def kernel(x_prompt, x_sample, cache_kv_latent, cache_k_rope, state_rnn_h, state_rnn_conv, state_ffn_conv,
           page_table, meta_tokens, norm_mix, w_in, rnn_conv_w, rnn_conv_b, rnn_wa, rnn_ba, rnn_wx, rnn_bx,
           rnn_lambda, q_norm, w_uq, kv_norm, w_uk, w_uv, w_br_rnn, w_br_attn, w_out, norm_ffn, ffn_w_up,
           ffn_conv_w, ffn_conv_b, ffn_w_down, norm_final):
    bp, seq, d = x_prompt.shape
    bs, ts, _ = x_sample.shape
    depth = w_in.shape[0]
    n_meta = meta_tokens.shape[0]
    kv_rank, heads, nope = w_uk.shape[1:]
    q_rank = w_uq.shape[1]
    rope = w_uq.shape[2] // heads - nope
    ff = ffn_w_down.shape[1]
    rnn_k = rnn_conv_w.shape[1]
    ffn_k = ffn_conv_w.shape[1]
    past_len = page_table.shape[1] * cache_kv_latent.shape[2]
    scale = 1.0 / math.sqrt(nope + rope)
    assert 2 * rope == LANES and kv_rank % LANES == 0 and nope % LANES == 0
    dims = (q_rank, kv_rank, heads, nope, rope)

    wts = _pack_weights(
        dict(norm_mix=norm_mix, w_in=w_in, rnn_conv_w=rnn_conv_w, rnn_conv_b=rnn_conv_b, rnn_wa=rnn_wa,
             rnn_ba=rnn_ba, rnn_wx=rnn_wx, rnn_bx=rnn_bx, rnn_lambda=rnn_lambda, q_norm=q_norm, w_uq=w_uq,
             kv_norm=kv_norm, w_uk=w_uk, w_uv=w_uv, w_br_rnn=w_br_rnn, w_br_attn=w_br_attn, w_out=w_out,
             norm_ffn=norm_ffn, ffn_w_up=ffn_w_up, ffn_conv_w=ffn_conv_w, ffn_conv_b=ffn_conv_b,
             ffn_w_down=ffn_w_down),
        d, q_rank, kv_rank, heads, nope, rope)

    lp = seq + n_meta
    lpad = _round_up(lp, PROMPT_TILE)
    xp = jnp.concatenate([jnp.broadcast_to(meta_tokens[None].astype(x_prompt.dtype), (bp, n_meta, d)), x_prompt,
                          jnp.zeros((bp, lpad - lp, d), x_prompt.dtype)], axis=1)
    cos_p, sin_p = _rope_tables(jnp.arange(lpad, dtype=jnp.int32), rope // 2)
    p_pad = SUBLANES
    zeros_p = lambda c: jnp.zeros((bp, p_pad, c), F32)
    rs = ts * bs
    xs = x_sample.transpose(1, 0, 2).reshape(1, rs, d)
    cos_s, sin_s = _rope_tables(past_len + jnp.repeat(jnp.arange(ts, dtype=jnp.int32), bs), rope // 2)
    t_pad = _round_up(ts, 2 * SUBLANES)
    cache_rt = cache_k_rope.transpose(0, 1, 3, 2)

    def time_major(state):
        return state.transpose(1, 0, 2).reshape(1, state.shape[1] * bs, state.shape[2])

    def batch_major(a, k):
        return a.reshape(k, bs, a.shape[-1]).transpose(1, 0, 2)

    outs = {k: [] for k in ("p_lat", "p_rope", "p_h", "p_rc", "p_fc", "s_lat", "s_rope", "s_h", "s_rc", "s_fc")}
    for l in range(depth):
        x_rnn, g_rnn, g_a, g_b, ckv, ckvb, kr, krb, qlat, qrope = _in_proj(xp, cos_p, sin_p, wts, l, PROMPT_TILE, dims)
        out_a, h_last = _rnn(x_rnn, g_rnn, zeros_p(d), jnp.zeros((bp, 1, d), F32), wts, l, PROMPT_TILE, 1, lp)
        o_lat = _prompt_attn(qlat, qrope, ckvb, krb, heads, PROMPT_TILE, ATTN_KEY_TILE, scale)
        x1 = _merge(o_lat, out_a, g_a, g_b, xp, wts, l, PROMPT_TILE)
        xp, u_last = _ffn(x1, zeros_p(ff), wts, l, PROMPT_TILE, 1, lp)
        outs["p_lat"].append(ckv[:, :lp])
        outs["p_rope"].append(kr[:, :lp])
        outs["p_h"].append(h_last[:, 0])
        outs["p_rc"].append(x_rnn[:, lp - (rnn_k - 1):lp])
        outs["p_fc"].append(u_last)

        x_rnn, g_rnn, g_a, g_b, ckv, ckvb, kr, krb, qlat, qrope = _in_proj(xs, cos_s, sin_s, wts, l, rs, dims)
        rc0 = time_major(state_rnn_conv[l])
        out_a, h_last = _rnn(x_rnn, g_rnn, _pad_rows(rc0, _round_up(rc0.shape[1], SUBLANES)),
                             state_rnn_h[l][None], wts, l, rs, bs, ts)
        to_seq = lambda a: jnp.pad(batch_major(a, ts), ((0, 0), (0, t_pad - ts), (0, 0)))
        o_seq = _sample_attn(page_table, batch_major(qlat, ts).reshape(bs, ts * heads, kv_rank),
                             batch_major(qrope, ts).reshape(bs, ts * heads, rope),
                             to_seq(ckv), to_seq(kr), cache_kv_latent, cache_rt, l, heads, scale)
        o_lat = o_seq.reshape(bs, ts, heads * kv_rank).transpose(1, 0, 2).reshape(1, rs, heads * kv_rank)
        x1 = _merge(o_lat, out_a, g_a, g_b, xs, wts, l, rs)
        fc0 = time_major(state_ffn_conv[l])
        xs, u_last = _ffn(x1, _pad_rows(fc0, _round_up(fc0.shape[1], SUBLANES)), wts, l, rs, bs, ts)
        outs["s_lat"].append(batch_major(ckv, ts))
        outs["s_rope"].append(batch_major(kr, ts))
        outs["s_h"].append(h_last[0])
        outs["s_rc"].append(batch_major(x_rnn[:, (ts - (rnn_k - 1)) * bs:], rnn_k - 1))
        outs["s_fc"].append(batch_major(u_last, ffn_k - 1))

    y_prompt = _final_norm(xp, norm_final[None], PROMPT_TILE)[:, n_meta:lp]
    y_sample = batch_major(_final_norm(xs, norm_final[None], rs), ts)
    st = {k: jnp.stack(v) for k, v in outs.items()}
    return (y_prompt, y_sample, st["p_lat"], st["p_rope"], st["p_h"], st["p_rc"], st["p_fc"],
            st["s_lat"], st["s_rope"], st["s_h"], st["s_rc"], st["s_fc"])
```

```python
import functools
import math

import jax
import jax.numpy as jnp
from jax import lax
from jax.experimental import pallas as pl
from jax.experimental.pallas import tpu as pltpu

F32 = jnp.float32
BF16 = jnp.bfloat16

EPS = 1e-6
RGLRU_C = 8.0
ROPE_THETA = 10000.0
LANES = 128
SUBLANES = 8
MXU_WIDTH = 256
VMEM_LIMIT = 56 * 1024 * 1024
LOG2E = 1.4426950408889634
PROMPT_TILE = 256
ATTN_KEY_TILE = 512
HEADS_AHEAD = 3
PAGES_PER_CHUNK = 8
CHUNKS_AHEAD = 2
DMA_UNROLL = 8
FFN_CHUNK = 512
RNN_PIECE = 32


def _round_up(x, m):
    return (x + m - 1) // m * m


def _rms(x, g):
    return x * lax.rsqrt(jnp.mean(x * x, axis=-1, keepdims=True) + EPS) * g


def _gelu(x):
    return 0.5 * x * (1.0 + jnp.tanh(math.sqrt(2.0 / math.pi) * (x + 0.044715 * (x * x * x))))


def _sigmoid(x):
    return 1.0 / (1.0 + jnp.exp(-x))


def _dot(a, b):
    return jnp.dot(a, b, preferred_element_type=F32)


def _dot_nt(a, b):
    return lax.dot_general(a, b, (((1,), (1,)), ((), ())), preferred_element_type=F32)


def _params(*sem):
    return pltpu.CompilerParams(dimension_semantics=sem, vmem_limit_bytes=VMEM_LIMIT)


def _row_spec(tile, cols):
    return pl.BlockSpec((None, tile, cols), lambda g, t: (g, t, 0))


def _stream_spec(rows, cols):
    return pl.BlockSpec((None, rows, cols), lambda g, t: (g, 0, 0))


def _layer_spec(layer, *shape):
    zeros = (0,) * len(shape)
    return pl.BlockSpec((None,) + tuple(shape), lambda g, t: (layer,) + zeros)


def _tail_rows(t_real, keep, step, tile, pad):
    first_row = (t_real - keep) * step
    last_tile = (first_row + keep * step - 1) // tile
    off = pad + first_row - last_tile * tile
    assert off >= 0
    return last_tile, off


def _scan_rows(a, u, step):
    n = a.shape[0]
    row = lax.broadcasted_iota(jnp.int32, a.shape, 0)
    d = step
    while d < n:
        keep = row >= d
        a_prev = jnp.where(keep, pltpu.roll(a, d, 0), 1.0)
        u_prev = jnp.where(keep, pltpu.roll(u, d, 0), 0.0)
        u = a * u_prev + u
        a = a * a_prev
        d *= 2
    return a, u


def _recurrence(a, u, h_in, step):
    n, width = a.shape
    if step > 1:
        a_p, u_p = _scan_rows(a, u, step)
        return a_p * jnp.concatenate([h_in] * (n // step), axis=0) + u_p
    groups = n // SUBLANES
    a3 = a.reshape(groups, SUBLANES, width)
    u3 = u.reshape(groups, SUBLANES, width)
    sub = lax.broadcasted_iota(jnp.int32, a3.shape, 1)
    d = 1
    while d < SUBLANES:
        keep = sub >= d
        a_prev = jnp.where(keep, pltpu.roll(a3, d, 1), 1.0)
        u_prev = jnp.where(keep, pltpu.roll(u3, d, 1), 0.0)
        u3 = a3 * u_prev + u3
        a3 = a3 * a_prev
        d *= 2
    carry = h_in
    out = []
    for g in range(groups):
        h = a3[g] * carry + u3[g]
        out.append(h)
        carry = h[SUBLANES - 1:SUBLANES, :]
    return jnp.concatenate(out, axis=0)


def _mix_in_kernel(x_ref, nm_ref, w_ref, qn_ref, kvn_ref, wuq_ref, wuk_ref, cos_ref, sin_ref,
                   st_ref, h0_ref, cw_ref, cb_ref, wg_ref, ba_ref, bx_ref, lam_ref, wbr_ref,
                   outa_ref, ga_ref, gb_ref, ckv_ref, ckvb_ref, kr_ref, krb_ref, qlat_ref, qrope_ref,
                   hlast_ref, xlast_ref, xbuf, gbuf, xcbuf, gbuf2, hcar, gated, cqn,
                   *, tile, pad, step, kconv, blk, d, q_rank, kv_rank, heads, nope, rope,
                   h_tile, h_off, x_tile, x_off):
    t = pl.program_id(1)

    @pl.when(t == 0)
    def _():
        xbuf[0:pad, :] = st_ref[...]
        hcar[...] = h0_ref[...]

    h = _rms(x_ref[...], nm_ref[...]).astype(BF16)

    def proj(c0, c1):
        return _dot(h, w_ref[:, c0:c1])

    xbuf[pad:pad + tile, :] = proj(0, d)
    gbuf[...] = proj(d, 2 * d)
    cos = cos_ref[...]
    sin = sin_ref[...]

    for n in range(d // blk):
        cs = slice(n * blk, (n + 1) * blk)
        xc = cb_ref[:, cs] + cw_ref[kconv - 1:kconv, cs] * xbuf[pad:pad + tile, cs]
        for k in range(kconv - 1):
            off = pad - (kconv - 1 - k) * step
            xc = xc + cw_ref[k:k + 1, cs] * xbuf[off:off + tile, cs]
        xcbuf[:, cs] = xc
        gbuf2[:, 2 * n * blk:2 * (n + 1) * blk] = _dot(xc.astype(BF16), wg_ref[n])

    piece = RNN_PIECE if step == 1 else tile

    def rnn_piece(n, rc):
        def run():
            cs = slice(n * blk, (n + 1) * blk)
            rs = slice(rc * piece, (rc + 1) * piece)
            xc = xcbuf[rs, cs]
            r = _sigmoid(gbuf2[rs, 2 * n * blk:(2 * n + 1) * blk] + ba_ref[:, cs])
            i = _sigmoid(gbuf2[rs, (2 * n + 1) * blk:2 * (n + 1) * blk] + bx_ref[:, cs])
            neg_lam = -lam_ref[:, cs]
            softplus = jnp.maximum(neg_lam, 0.0) + jnp.log(1.0 + jnp.exp(-jnp.abs(neg_lam)))
            log_a = -RGLRU_C * r * softplus
            a = jnp.exp(log_a)
            y = 1.0 - jnp.exp(2.0 * log_a)
            root = jnp.where(y > 0.0, y * lax.rsqrt(y), 0.0)
            u = root * (i * xc)
            hs = _recurrence(a, u, hcar[:, cs], step)
            hcar[:, cs] = hs[piece - step:piece, :]
            if rc == h_off // piece:
                @pl.when(t == h_tile)
                def _():
                    hlast_ref[:, cs] = hs[h_off % piece:h_off % piece + step, :]

            gated[rs, cs] = (hs * _gelu(gbuf[rs, cs])).astype(BF16)
        return run

    def gate_cols(ref, c0, c):
        def run():
            ref[:, c:c + MXU_WIDTH] = proj(c0 + c, c0 + c + MXU_WIDTH)
        return run

    def latent_and_key():
        o = 4 * d
        cqn[...] = _rms(proj(o, o + q_rank), qn_ref[...]).astype(BF16)
        o += q_rank
        c_kv = _rms(proj(o, o + kv_rank), kvn_ref[...])
        o += kv_rank
        ckv_ref[...] = c_kv
        ckvb_ref[...] = c_kv.astype(BF16)
        kk = proj(o, o + 2 * rope)
        k_rot = (kk * cos + pltpu.roll(kk, rope, 1) * sin)[:, :rope]
        kr_ref[...] = k_rot
        krb_ref[...] = k_rot.astype(BF16)

    hn = heads * nope
    hr = heads * rope

    def q_rope_cols(j):
        def run():
            q = cqn[...]
            qr = _dot(q, wuq_ref[:, hn + j * LANES:hn + (j + 1) * LANES])
            qp = _dot(q, wuq_ref[:, hn + hr + j * LANES:hn + hr + (j + 1) * LANES])
            qrope_ref[:, j * LANES:(j + 1) * LANES] = (qr * cos + qp * sin).astype(BF16)
        return run

    def q_lat_head(hd):
        def run():
            qn = _dot(cqn[...], wuq_ref[:, hd * nope:(hd + 1) * nope]).astype(BF16)
            qlat_ref[:, hd * kv_rank:(hd + 1) * kv_rank] = _dot(qn, wuk_ref[hd]).astype(BF16)
        return run

    others = [latent_and_key]
    others += [gate_cols(ga_ref, 2 * d, c) for c in range(0, d, MXU_WIDTH)]
    others += [gate_cols(gb_ref, 3 * d, c) for c in range(0, d, MXU_WIDTH)]
    others += [q_rope_cols(j) for j in range(hr // LANES)] + [q_lat_head(hd) for hd in range(heads)]
    n_blocks = d // blk
    n_pieces = tile // piece
    per_dot = MXU_WIDTH // blk
    vec = [rnn_piece(n, rc) for n in range(n_blocks) for rc in range(n_pieces)]
    out_a = None
    for k, item in enumerate(vec):
        item()
        for other in others[k * len(others) // len(vec):(k + 1) * len(others) // len(vec)]:
            other()
        if (k + 1) % (n_pieces * per_dot) == 0:
            n_done = (k + 1) // n_pieces
            ks = slice((n_done - per_dot) * blk, n_done * blk)
            part = _dot(gated[:, ks], wbr_ref[ks, :])
            out_a = part if out_a is None else out_a + part
    outa_ref[...] = out_a

    @pl.when(t == x_tile)
    def _():
        xlast_ref[...] = xbuf[x_off:x_off + (kconv - 1) * step, :]

    xbuf[0:pad, :] = xbuf[tile:tile + pad, :]


def _mix_in(x, cos4, sin4, conv_state, h0, wts, layer, tile, step, t_real, dims):
    g_, r_, d = x.shape
    q_rank, kv_rank, heads, nope, rope = dims
    width = wts["w_in"].shape[-1]
    kconv = wts["rnn_conv_w"].shape[1]
    nb, blk = wts["rnn_wg"].shape[1], wts["rnn_wg"].shape[2]
    pad = conv_state.shape[1]
    keep = (kconv - 1) * step
    assert pad % SUBLANES == 0 and pad >= keep and tile >= pad and blk == LANES and tile % SUBLANES == 0
    h_tile, h_off = _tail_rows(t_real, 1, step, tile, 0)
    x_tile, x_off = _tail_rows(t_real, kconv - 1, step, tile, pad)
    f = functools.partial(_mix_in_kernel, tile=tile, pad=pad, step=step, kconv=kconv, blk=blk, d=d,
                          q_rank=q_rank, kv_rank=kv_rank, heads=heads, nope=nope, rope=rope,
                          h_tile=h_tile, h_off=h_off, x_tile=x_tile, x_off=x_off)
    tab = pl.BlockSpec((tile, LANES), lambda g, t: (t, 0))
    sds = jax.ShapeDtypeStruct
    out_shape = (
        sds((g_, r_, d), F32), sds((g_, r_, d), F32), sds((g_, r_, d), F32),
        sds((g_, r_, kv_rank), F32), sds((g_, r_, kv_rank), BF16), sds((g_, r_, rope), F32), sds((g_, r_, rope), BF16),
        sds((g_, r_, heads * kv_rank), BF16), sds((g_, r_, heads * rope), BF16),
        sds((g_, step, d), F32), sds((g_, keep, d), F32),
    )
    out_specs = (
        _row_spec(tile, d), _row_spec(tile, d), _row_spec(tile, d),
        _row_spec(tile, kv_rank), _row_spec(tile, kv_rank), _row_spec(tile, rope), _row_spec(tile, rope),
        _row_spec(tile, heads * kv_rank), _row_spec(tile, heads * rope),
        _stream_spec(step, d), _stream_spec(keep, d),
    )
    in_specs = [
        _row_spec(tile, d), _layer_spec(layer, 1, d), _layer_spec(layer, d, width),
        _layer_spec(layer, 1, q_rank), _layer_spec(layer, 1, kv_rank),
        _layer_spec(layer, q_rank, wts["w_uq"].shape[-1]), _layer_spec(layer, heads, nope, kv_rank),
        tab, tab, _stream_spec(pad, d), _stream_spec(step, d),
        _layer_spec(layer, kconv, d), _layer_spec(layer, 1, d), _layer_spec(layer, nb, blk, 2 * blk),
        _layer_spec(layer, 1, d), _layer_spec(layer, 1, d), _layer_spec(layer, 1, d), _layer_spec(layer, d, d),
    ]
    return pl.pallas_call(
        f, grid=(g_, r_ // tile), in_specs=in_specs, out_specs=out_specs, out_shape=out_shape,
        scratch_shapes=[pltpu.VMEM((pad + tile, d), F32), pltpu.VMEM((tile, d), F32), pltpu.VMEM((tile, d), F32),
                        pltpu.VMEM((tile, 2 * d), F32), pltpu.VMEM((step, d), F32),
                        pltpu.VMEM((tile, d), BF16), pltpu.VMEM((tile, q_rank), BF16)],
        compiler_params=_params("parallel", "arbitrary"), name="mix_in",
    )(x, wts["norm_mix"], wts["w_in"], wts["q_norm"], wts["kv_norm"], wts["w_uq"], wts["w_uk"], cos4, sin4,
      conv_state, h0, wts["rnn_conv_w"], wts["rnn_conv_b"], wts["rnn_wg"], wts["rnn_ba"], wts["rnn_bx"],
      wts["rnn_lambda"], wts["w_br_rnn"])


def _prompt_attn_kernel(ql_ref, qr_ref, c_ref, kr_ref, o_ref, m_sc, l_sc, acc_sc, *, tq, tk, heads, kv, rope, coef):
    i = pl.program_id(1)
    m_sc[...] = jnp.full(m_sc.shape, -jnp.inf, F32)
    l_sc[...] = jnp.zeros(l_sc.shape, F32)
    acc_sc[...] = jnp.zeros(acc_sc.shape, F32)

    def step(k0, width, masked):
        c = c_ref[pl.ds(k0, width), :]
        kr = kr_ref[pl.ds(k0, width), :]
        if masked:
            visible = (lax.broadcasted_iota(jnp.int32, (tq, width), 1)
                       <= lax.broadcasted_iota(jnp.int32, (tq, width), 0))

        def scores(h):
            return _dot_nt(ql_ref[:, h * kv:(h + 1) * kv], c) + _dot_nt(qr_ref[:, h * rope:(h + 1) * rope], kr)

        pend = [scores(h) for h in range(min(HEADS_AHEAD, heads))]
        for h in range(heads):
            s = pend.pop(0)
            if h + HEADS_AHEAD < heads:
                pend.append(scores(h + HEADS_AHEAD))
            if masked:
                s = jnp.where(visible, s, -jnp.inf)
            m_prev = m_sc[h]
            m_new = jnp.maximum(m_prev, jnp.max(s, axis=1, keepdims=True))
            alpha = jnp.exp2((m_prev - m_new) * coef)
            p = jnp.exp2((s - jnp.concatenate([m_new] * (width // LANES), axis=1)) * coef)
            psum = p[:, :LANES]
            for t in range(1, width // LANES):
                psum = psum + p[:, t * LANES:(t + 1) * LANES]
            l_sc[h] = alpha * l_sc[h] + psum
            acc_sc[h] = jnp.concatenate([alpha] * (kv // LANES), axis=1) * acc_sc[h] + _dot(p.astype(BF16), c)
            m_sc[h] = m_new

    ratio = tk // tq
    n_wide = i // ratio

    def wide_body(j, carry):
        step(pl.multiple_of(j * tk, tk), tk, False)
        return carry

    def narrow_body(j, carry):
        step(pl.multiple_of(j * tq, tq), tq, False)
        return carry

    lax.fori_loop(0, n_wide, wide_body, 0)
    lax.fori_loop(n_wide * ratio, i, narrow_body, 0)
    step(pl.multiple_of(i * tq, tq), tq, True)
    for h in range(heads):
        l = jnp.sum(l_sc[h], axis=1, keepdims=True)
        o_ref[:, h * kv:(h + 1) * kv] = (acc_sc[h] / l).astype(BF16)


def _prompt_attn(qlat, qrope, ckvb, krb, heads, tq, tk, scale):
    g_, r_, kv_rank = ckvb.shape
    rope = krb.shape[-1]
    assert tk % tq == 0 and r_ % tq == 0 and tq % LANES == 0
    f = functools.partial(_prompt_attn_kernel, tq=tq, tk=tk, heads=heads, kv=kv_rank, rope=rope,
                          coef=scale * LOG2E)
    return pl.pallas_call(
        f, grid=(g_, r_ // tq),
        in_specs=[_row_spec(tq, heads * kv_rank), _row_spec(tq, heads * rope),
                  _stream_spec(r_, kv_rank), _stream_spec(r_, rope)],
        out_specs=_row_spec(tq, heads * kv_rank),
        out_shape=jax.ShapeDtypeStruct((g_, r_, heads * kv_rank), BF16),
        scratch_shapes=[pltpu.VMEM((heads, tq, LANES), F32), pltpu.VMEM((heads, tq, LANES), F32),
                        pltpu.VMEM((heads, tq, kv_rank), F32)],
        compiler_params=_params("parallel", "parallel"), name="prompt_attn",
    )(qlat, qrope, ckvb, krb)


def _sample_attn_kernel(pt_ref, ql_ref, qr_ref, cn_ref, kn_ref, cache_c, cache_rt, o_ref,
                        cbuf, rbuf, sem_c, sem_r, *, layer, n_pages, page, heads, t_new, cpages, coef):
    b = pl.program_id(0)
    nb = pl.num_programs(0)

    def page_copies(seq, slot, p):
        pg = pt_ref[seq * n_pages + p]
        return (pltpu.make_async_copy(cache_c.at[layer, pg], cbuf.at[slot, p], sem_c.at[slot]),
                pltpu.make_async_copy(cache_rt.at[layer, pg], rbuf.at[slot, p], sem_r.at[slot]))

    def start_fetch(seq, slot):
        def body(p, carry):
            for cp in page_copies(seq, slot, p):
                cp.start()
            return carry
        lax.fori_loop(0, n_pages, body, 0, unroll=DMA_UNROLL)

    def wait_fetch(seq, slot):
        def body(p, carry):
            for cp in page_copies(seq, slot, p):
                cp.wait()
            return carry
        lax.fori_loop(0, n_pages, body, 0, unroll=DMA_UNROLL)

    slot = lax.rem(b, 2)

    @pl.when(b == 0)
    def _():
        start_fetch(b, slot)

    @pl.when(b + 1 < nb)
    def _():
        start_fetch(b + 1, 1 - slot)

    wait_fetch(b, slot)

    ql = ql_ref[...]
    qr = qr_ref[...]
    kv = ql.shape[1]
    ck = cpages * page
    n_chunks = n_pages // cpages

    def scores(j):
        c = cbuf[slot, j * cpages:(j + 1) * cpages].reshape(ck, kv).astype(BF16)
        s_rope = jnp.concatenate([_dot(qr, rbuf[slot, j * cpages + p].astype(BF16)) for p in range(cpages)], axis=1)
        return _dot_nt(ql, c) + s_rope, c

    cn = cn_ref[...].astype(BF16)
    kn = kn_ref[...].astype(BF16)
    s_new = _dot_nt(ql, cn) + _dot_nt(qr, kn)
    q_tok = lax.broadcasted_iota(jnp.int32, s_new.shape, 0) // heads
    k_tok = lax.broadcasted_iota(jnp.int32, s_new.shape, 1)
    s_new = jnp.where((k_tok <= q_tok) & (k_tok < t_new), s_new, -jnp.inf)
    m = jnp.max(s_new, axis=1, keepdims=True)
    p_new = jnp.exp2((s_new - m) * coef)
    l = jnp.sum(p_new, axis=1, keepdims=True)
    acc = _dot(p_new.astype(BF16), cn)

    pend = [scores(j) for j in range(min(CHUNKS_AHEAD, n_chunks))]
    for j in range(n_chunks):
        s, c = pend.pop(0)
        if j + CHUNKS_AHEAD < n_chunks:
            pend.append(scores(j + CHUNKS_AHEAD))
        m_new = jnp.maximum(m, jnp.max(s, axis=1, keepdims=True))
        alpha = jnp.exp2((m - m_new) * coef)
        p = jnp.exp2((s - m_new) * coef)
        l = alpha * l + jnp.sum(p, axis=1, keepdims=True)
        acc = alpha * acc + _dot(p.astype(BF16), c)
        m = m_new
    o_ref[...] = (acc / l).astype(BF16)


def _sample_attn(page_table, qlat, qrope, c_new, k_new, cache_c, cache_rt, layer, heads, scale):
    bs, rows, kv_rank = qlat.shape
    rope = qrope.shape[-1]
    t_pad = c_new.shape[1]
    n_pages = page_table.shape[1]
    page = cache_c.shape[2]
    cpages = math.gcd(PAGES_PER_CHUNK, n_pages)
    f = functools.partial(_sample_attn_kernel, layer=layer, n_pages=n_pages, page=page, heads=heads,
                          t_new=rows // heads, cpages=cpages, coef=scale * LOG2E)
    seq = lambda r, c: pl.BlockSpec((None, r, c), lambda b, pt: (b, 0, 0))
    grid_spec = pltpu.PrefetchScalarGridSpec(
        num_scalar_prefetch=1, grid=(bs,),
        in_specs=[seq(rows, kv_rank), seq(rows, rope), seq(t_pad, kv_rank), seq(t_pad, rope),
                  pl.BlockSpec(memory_space=pl.ANY), pl.BlockSpec(memory_space=pl.ANY)],
        out_specs=seq(rows, kv_rank),
        scratch_shapes=[pltpu.VMEM((2, n_pages, page, kv_rank), F32), pltpu.VMEM((2, n_pages, rope, page), F32),
                        pltpu.SemaphoreType.DMA((2,)), pltpu.SemaphoreType.DMA((2,))],
    )
    return pl.pallas_call(
        f, grid_spec=grid_spec, out_shape=jax.ShapeDtypeStruct((bs, rows, kv_rank), BF16),
        compiler_params=_params("arbitrary"), name="sample_attn",
    )(page_table.reshape(-1), qlat, qrope, c_new, k_new, cache_c, cache_rt)


def _mix_out_kernel(o_ref, a_ref, ga_ref, gb_ref, x_ref, wuv_ref, wbr_ref, wout_ref,
                    nf_ref, st_ref, wup_ref, cw_ref, cb_ref, wdn_ref, nfin_ref,
                    x2_ref, ulast_ref, y_ref, ov, ubuf,
                    *, heads, kv_rank, v_dim, tile, pad, step, kconv, ff, fc, u_tile, u_off):
    t = pl.program_id(1)

    @pl.when(t == 0)
    def _():
        ubuf[0:pad, :] = st_ref[...]

    for hd in range(heads):
        ov[:, hd * v_dim:(hd + 1) * v_dim] = _dot(o_ref[:, hd * kv_rank:(hd + 1) * kv_rank], wuv_ref[hd]).astype(BF16)
    out_b = _dot(ov[...], wbr_ref[...])
    merged = _sigmoid(ga_ref[...]) * a_ref[...] + _sigmoid(gb_ref[...]) * out_b
    x1 = x_ref[...] + _dot(merged.astype(BF16), wout_ref[...])

    hf = _rms(x1, nf_ref[...]).astype(BF16)
    acc = x1
    for c in range(ff // fc):
        cs = slice(c * fc, (c + 1) * fc)
        u = _dot(hf, wup_ref[:, cs])
        v = _dot(hf, wup_ref[:, ff + c * fc:ff + (c + 1) * fc])
        ubuf[pad:pad + tile, cs] = u
        uc = cb_ref[:, cs] + cw_ref[kconv - 1:kconv, cs] * u
        for k in range(kconv - 1):
            off = pad - (kconv - 1 - k) * step
            uc = uc + cw_ref[k:k + 1, cs] * ubuf[off:off + tile, cs]
        acc = acc + _dot((_gelu(uc) * v).astype(BF16), wdn_ref[cs, :])
    x2_ref[...] = acc
    if y_ref is not None:
        y_ref[...] = _rms(acc, nfin_ref[...])

    @pl.when(t == u_tile)
    def _():
        ulast_ref[...] = ubuf[u_off:u_off + (kconv - 1) * step, :]

    ubuf[0:pad, :] = ubuf[tile:tile + pad, :]


def _mix_out(o_lat, out_a, g_a, g_b, x, conv_state, norm_final, wts, layer, tile, step, t_real, final):
    g_, r_, d = x.shape
    heads, kv_rank, v_dim = wts["w_uv"].shape[1:]
    kconv, ff = wts["ffn_conv_w"].shape[1:]
    pad = conv_state.shape[1]
    keep = (kconv - 1) * step
    fc = math.gcd(FFN_CHUNK, ff)
    assert pad % SUBLANES == 0 and pad >= keep and tile >= pad
    u_tile, u_off = _tail_rows(t_real, kconv - 1, step, tile, pad)
    body = functools.partial(_mix_out_kernel, heads=heads, kv_rank=kv_rank, v_dim=v_dim, tile=tile, pad=pad,
                             step=step, kconv=kconv, ff=ff, fc=fc, u_tile=u_tile, u_off=u_off)
    n_in = 15
    if final:
        f = body
    else:
        def f(*refs):
            return body(*refs[:n_in + 2], None, *refs[n_in + 2:])
    sds = jax.ShapeDtypeStruct
    out_specs = [_row_spec(tile, d), _stream_spec(keep, ff)]
    out_shape = [sds((g_, r_, d), F32), sds((g_, keep, ff), F32)]
    if final:
        out_specs.append(_row_spec(tile, d))
        out_shape.append(sds((g_, r_, d), F32))
    return pl.pallas_call(
        f, grid=(g_, r_ // tile),
        in_specs=[_row_spec(tile, heads * kv_rank), _row_spec(tile, d), _row_spec(tile, d), _row_spec(tile, d),
                  _row_spec(tile, d), _layer_spec(layer, heads, kv_rank, v_dim),
                  _layer_spec(layer, heads * v_dim, d), _layer_spec(layer, d, d),
                  _layer_spec(layer, 1, d), _stream_spec(pad, ff),
                  _layer_spec(layer, d, 2 * ff), _layer_spec(layer, kconv, ff), _layer_spec(layer, 1, ff),
                  _layer_spec(layer, ff, d), pl.BlockSpec((1, d), lambda g, t: (0, 0))],
        out_specs=tuple(out_specs), out_shape=tuple(out_shape),
        scratch_shapes=[pltpu.VMEM((tile, heads * v_dim), BF16), pltpu.VMEM((pad + tile, ff), F32)],
        compiler_params=_params("parallel", "arbitrary"), name="mix_out",
    )(o_lat, out_a, g_a, g_b, x, wts["w_uv"], wts["w_br_attn"], wts["w_out"], wts["norm_ffn"], conv_state,
      wts["ffn_w_up"], wts["ffn_conv_w"], wts["ffn_conv_b"], wts["ffn_w_down"], norm_final)


def _rope_tables(pos, half):
    inv = ROPE_THETA ** (-jnp.arange(half, dtype=F32) / half)
    ang = pos.astype(F32)[:, None] * inv[None, :]
    c, s = jnp.cos(ang), jnp.sin(ang)
    return jnp.concatenate([c, c, c, c], axis=1), jnp.concatenate([-s, s, -s, s], axis=1)


def _swap_halves(w):
    half = w.shape[-1] // 2
    return jnp.concatenate([w[..., half:], w[..., :half]], axis=-1)


def _pack_weights(p, d, q_rank, kv_rank, heads, nope, rope):
    depth = p["w_in"].shape[0]
    w_in = p["w_in"]
    o_cq = 2 * d
    o_kr = o_cq + q_rank + kv_rank
    o_ga = o_kr + rope
    k_r = w_in[:, :, o_kr:o_ga]
    w_in_p = jnp.concatenate([w_in[:, :, :2 * d], w_in[:, :, o_ga:o_ga + 2 * d], w_in[:, :, o_cq:o_kr],
                              k_r, _swap_halves(k_r)], axis=-1).astype(BF16)
    w_uq = p["w_uq"].reshape(depth, q_rank, heads, nope + rope)
    q_r = w_uq[..., nope:]
    w_uq_p = jnp.concatenate([w_uq[..., :nope].reshape(depth, q_rank, heads * nope),
                              q_r.reshape(depth, q_rank, heads * rope),
                              _swap_halves(q_r).reshape(depth, q_rank, heads * rope)], axis=-1).astype(BF16)
    row = lambda a: a[:, None, :]
    return {
        "norm_mix": row(p["norm_mix"]), "w_in": w_in_p, "q_norm": row(p["q_norm"]), "kv_norm": row(p["kv_norm"]),
        "w_uq": w_uq_p, "w_uk": p["w_uk"].transpose(0, 2, 3, 1).astype(BF16),
        "w_uv": p["w_uv"].transpose(0, 2, 1, 3).astype(BF16),
        "rnn_conv_w": p["rnn_conv_w"], "rnn_conv_b": row(p["rnn_conv_b"]),
        "rnn_wg": jnp.concatenate([p["rnn_wa"], p["rnn_wx"]], axis=-1).astype(BF16),
        "rnn_ba": row(p["rnn_ba"]), "rnn_bx": row(p["rnn_bx"]), "rnn_lambda": row(p["rnn_lambda"]),
        "w_br_rnn": p["w_br_rnn"].astype(BF16), "w_br_attn": p["w_br_attn"].astype(BF16),
        "w_out": p["w_out"].astype(BF16), "norm_ffn": row(p["norm_ffn"]),
        "ffn_w_up": p["ffn_w_up"].astype(BF16), "ffn_conv_w": p["ffn_conv_w"],
        "ffn_conv_b": row(p["ffn_conv_b"]), "ffn_w_down": p["ffn_w_down"].astype(BF16),
    }


def _pad_rows(a, rows):
    return jnp.pad(a, ((0, 0), (rows - a.shape[1], 0), (0, 0)))


def kernel(x_prompt, x_sample, cache_kv_latent, cache_k_rope, state_rnn_h, state_rnn_conv, state_ffn_conv,
           page_table, meta_tokens, norm_mix, w_in, rnn_conv_w, rnn_conv_b, rnn_wa, rnn_ba, rnn_wx, rnn_bx,
           rnn_lambda, q_norm, w_uq, kv_norm, w_uk, w_uv, w_br_rnn, w_br_attn, w_out, norm_ffn, ffn_w_up,
           ffn_conv_w, ffn_conv_b, ffn_w_down, norm_final):
    bp, seq, d = x_prompt.shape
    bs, ts, _ = x_sample.shape
    depth = w_in.shape[0]
    n_meta = meta_tokens.shape[0]
    kv_rank, heads, nope = w_uk.shape[1:]
    q_rank = w_uq.shape[1]
    rope = w_uq.shape[2] // heads - nope
    ff = ffn_w_down.shape[1]
    rnn_k = rnn_conv_w.shape[1]
    ffn_k = ffn_conv_w.shape[1]
    past_len = page_table.shape[1] * cache_kv_latent.shape[2]
    scale = 1.0 / math.sqrt(nope + rope)
    assert 2 * rope == LANES and kv_rank % LANES == 0 and nope % LANES == 0 and d % MXU_WIDTH == 0
    dims = (q_rank, kv_rank, heads, nope, rope)

    wts = _pack_weights(
        dict(norm_mix=norm_mix, w_in=w_in, rnn_conv_w=rnn_conv_w, rnn_conv_b=rnn_conv_b, rnn_wa=rnn_wa,
             rnn_ba=rnn_ba, rnn_wx=rnn_wx, rnn_bx=rnn_bx, rnn_lambda=rnn_lambda, q_norm=q_norm, w_uq=w_uq,
             kv_norm=kv_norm, w_uk=w_uk, w_uv=w_uv, w_br_rnn=w_br_rnn, w_br_attn=w_br_attn, w_out=w_out,
             norm_ffn=norm_ffn, ffn_w_up=ffn_w_up, ffn_conv_w=ffn_conv_w, ffn_conv_b=ffn_conv_b,
             ffn_w_down=ffn_w_down),
        d, q_rank, kv_rank, heads, nope, rope)
    nfin = norm_final[None]

    lp = seq + n_meta
    lpad = _round_up(lp, PROMPT_TILE)
    xp = jnp.concatenate([jnp.broadcast_to(meta_tokens[None].astype(x_prompt.dtype), (bp, n_meta, d)), x_prompt,
                          jnp.zeros((bp, lpad - lp, d), x_prompt.dtype)], axis=1)
    cos_p, sin_p = _rope_tables(jnp.arange(lpad, dtype=jnp.int32), rope // 2)
    zeros_p = lambda c: jnp.zeros((bp, SUBLANES, c), F32)
    rs = ts * bs
    xs = x_sample.transpose(1, 0, 2).reshape(1, rs, d)
    cos_s, sin_s = _rope_tables(past_len + jnp.repeat(jnp.arange(ts, dtype=jnp.int32), bs), rope // 2)
    t_pad = _round_up(ts, 2 * SUBLANES)
    cache_rt = cache_k_rope.transpose(0, 1, 3, 2)

    def time_major(state):
        return state.transpose(1, 0, 2).reshape(1, state.shape[1] * bs, state.shape[2])

    def batch_major(a, k):
        return a.reshape(k, bs, a.shape[-1]).transpose(1, 0, 2)

    outs = {k: [] for k in ("p_lat", "p_rope", "p_h", "p_rc", "p_fc", "s_lat", "s_rope", "s_h", "s_rc", "s_fc")}
    y_p = y_s = None
    for l in range(depth):
        final = l == depth - 1
        out_a, g_a, g_b, ckv, ckvb, kr, krb, qlat, qrope, h_last, x_last = _mix_in(
            xp, cos_p, sin_p, zeros_p(d), jnp.zeros((bp, 1, d), F32), wts, l, PROMPT_TILE, 1, lp, dims)
        o_lat = _prompt_attn(qlat, qrope, ckvb, krb, heads, PROMPT_TILE, ATTN_KEY_TILE, scale)
        res = _mix_out(o_lat, out_a, g_a, g_b, xp, zeros_p(ff), nfin, wts, l, PROMPT_TILE, 1, lp, final)
        xp, u_last = res[0], res[1]
        if final:
            y_p = res[2]
        outs["p_lat"].append(ckv[:, :lp])
        outs["p_rope"].append(kr[:, :lp])
        outs["p_h"].append(h_last[:, 0])
        outs["p_rc"].append(x_last)
        outs["p_fc"].append(u_last)

        rc0 = time_major(state_rnn_conv[l])
        out_a, g_a, g_b, ckv, ckvb, kr, krb, qlat, qrope, h_last, x_last = _mix_in(
            xs, cos_s, sin_s, _pad_rows(rc0, _round_up(rc0.shape[1], SUBLANES)), state_rnn_h[l][None],
            wts, l, rs, bs, ts, dims)
        to_seq = lambda a: jnp.pad(batch_major(a, ts), ((0, 0), (0, t_pad - ts), (0, 0)))
        o_seq = _sample_attn(page_table, batch_major(qlat, ts).reshape(bs, ts * heads, kv_rank),
                             batch_major(qrope, ts).reshape(bs, ts * heads, rope),
                             to_seq(ckv), to_seq(kr), cache_kv_latent, cache_rt, l, heads, scale)
        o_lat = o_seq.reshape(bs, ts, heads * kv_rank).transpose(1, 0, 2).reshape(1, rs, heads * kv_rank)
        fc0 = time_major(state_ffn_conv[l])
        res = _mix_out(o_lat, out_a, g_a, g_b, xs, _pad_rows(fc0, _round_up(fc0.shape[1], SUBLANES)), nfin,
                       wts, l, rs, bs, ts, final)
        xs, u_last = res[0], res[1]
        if final:
            y_s = res[2]
        outs["s_lat"].append(batch_major(ckv, ts))
        outs["s_rope"].append(batch_major(kr, ts))
        outs["s_h"].append(h_last[0])
        outs["s_rc"].append(batch_major(x_last, rnn_k - 1))
        outs["s_fc"].append(batch_major(u_last, ffn_k - 1))

    y_prompt = y_p[:, n_meta:lp]
    y_sample = batch_major(y_s, ts)
    st = {k: jnp.stack(v) for k, v in outs.items()}
    return (y_prompt, y_sample, st["p_lat"], st["p_rope"], st["p_h"], st["p_rc"], st["p_fc"],
            st["s_lat"], st["s_rope"], st["s_h"], st["s_rc"], st["s_fc"])
```

```python
import functools
import math

import jax
import jax.numpy as jnp
from jax import lax
from jax.experimental import pallas as pl
from jax.experimental.pallas import tpu as pltpu

F32 = jnp.float32
BF16 = jnp.bfloat16

EPS = 1e-6
RGLRU_C = 8.0
ROPE_THETA = 10000.0
LANES = 128
SUBLANES = 8
MXU_WIDTH = 256
VMEM_LIMIT = 56 * 1024 * 1024
LOG2E = 1.4426950408889634
PROMPT_TILE = 256
ATTN_KEY_TILES = (1024, 512)
HEADS_AHEAD = 3
PAGES_PER_CHUNK = 8
CHUNKS_AHEAD = 2
DMA_UNROLL = 8
FFN_CHUNK = 512
RNN_PIECE = 32
FFN_AHEAD = 2


def _round_up(x, m):
    return (x + m - 1) // m * m


def _rms(x, g):
    return x * lax.rsqrt(jnp.mean(x * x, axis=-1, keepdims=True) + EPS) * g


def _gelu(x):
    return 0.5 * x * (1.0 + jnp.tanh(math.sqrt(2.0 / math.pi) * (x + 0.044715 * (x * x * x))))


def _sigmoid(x):
    return 1.0 / (1.0 + jnp.exp(-x))


def _dot(a, b):
    return jnp.dot(a, b, preferred_element_type=F32)


def _dot_nt(a, b):
    return lax.dot_general(a, b, (((1,), (1,)), ((), ())), preferred_element_type=F32)


def _params(*sem):
    return pltpu.CompilerParams(dimension_semantics=sem, vmem_limit_bytes=VMEM_LIMIT)


def _row_spec(tile, cols):
    return pl.BlockSpec((None, tile, cols), lambda g, t: (g, t, 0))


def _stream_spec(rows, cols):
    return pl.BlockSpec((None, rows, cols), lambda g, t: (g, 0, 0))


def _layer_spec(layer, *shape):
    zeros = (0,) * len(shape)
    return pl.BlockSpec((None,) + tuple(shape), lambda g, t: (layer,) + zeros)


def _tail_rows(t_real, keep, step, tile, pad):
    first_row = (t_real - keep) * step
    last_tile = (first_row + keep * step - 1) // tile
    off = pad + first_row - last_tile * tile
    assert off >= 0
    return last_tile, off


def _scan_rows(a, u, step):
    n = a.shape[0]
    row = lax.broadcasted_iota(jnp.int32, a.shape, 0)
    d = step
    while d < n:
        keep = row >= d
        a_prev = jnp.where(keep, pltpu.roll(a, d, 0), 1.0)
        u_prev = jnp.where(keep, pltpu.roll(u, d, 0), 0.0)
        u = a * u_prev + u
        a = a * a_prev
        d *= 2
    return a, u


def _recurrence(a, u, h_in, step):
    n, width = a.shape
    if step > 1:
        a_p, u_p = _scan_rows(a, u, step)
        return a_p * jnp.concatenate([h_in] * (n // step), axis=0) + u_p
    groups = n // SUBLANES
    a3 = a.reshape(groups, SUBLANES, width)
    u3 = u.reshape(groups, SUBLANES, width)
    sub = lax.broadcasted_iota(jnp.int32, a3.shape, 1)
    d = 1
    while d < SUBLANES:
        keep = sub >= d
        a_prev = jnp.where(keep, pltpu.roll(a3, d, 1), 1.0)
        u_prev = jnp.where(keep, pltpu.roll(u3, d, 1), 0.0)
        u3 = a3 * u_prev + u3
        a3 = a3 * a_prev
        d *= 2
    carry = h_in
    out = []
    for g in range(groups):
        h = a3[g] * carry + u3[g]
        out.append(h)
        carry = h[SUBLANES - 1:SUBLANES, :]
    return jnp.concatenate(out, axis=0)


def _mix_in_kernel(x_ref, nm_ref, w_ref, qn_ref, kvn_ref, wuq_ref, wuk_ref, cos_ref, sin_ref,
                   st_ref, h0_ref, cw_ref, cb_ref, wg_ref, ba_ref, bx_ref, lam_ref, wbr_ref,
                   outa_ref, ga_ref, gb_ref, ckv_ref, ckvb_ref, kr_ref, krb_ref, qlat_ref, qrope_ref,
                   hlast_ref, xlast_ref, xbuf, gbuf, xcbuf, gbuf2, hcar, gated, cqn, qnope,
                   *, tile, pad, step, kconv, blk, d, q_rank, kv_rank, heads, nope, rope,
                   h_tile, h_off, x_tile, x_off):
    t = pl.program_id(1)

    @pl.when(t == 0)
    def _():
        xbuf[0:pad, :] = st_ref[...]
        hcar[...] = h0_ref[...]

    h = _rms(x_ref[...], nm_ref[...]).astype(BF16)

    def proj(c0, c1):
        return _dot(h, w_ref[:, c0:c1])

    xbuf[pad:pad + tile, :] = proj(0, d)
    gbuf[...] = proj(d, 2 * d)
    cos = cos_ref[...]
    sin = sin_ref[...]

    for n in range(d // blk):
        cs = slice(n * blk, (n + 1) * blk)
        xc = cb_ref[:, cs] + cw_ref[kconv - 1:kconv, cs] * xbuf[pad:pad + tile, cs]
        for k in range(kconv - 1):
            off = pad - (kconv - 1 - k) * step
            xc = xc + cw_ref[k:k + 1, cs] * xbuf[off:off + tile, cs]
        xcbuf[:, cs] = xc
        gbuf2[:, 2 * n * blk:2 * (n + 1) * blk] = _dot(xc.astype(BF16), wg_ref[n])

    piece = RNN_PIECE if step == 1 else tile

    def rnn_piece(n, rc):
        def run():
            cs = slice(n * blk, (n + 1) * blk)
            rs = slice(rc * piece, (rc + 1) * piece)
            xc = xcbuf[rs, cs]
            r = _sigmoid(gbuf2[rs, 2 * n * blk:(2 * n + 1) * blk] + ba_ref[:, cs])
            i = _sigmoid(gbuf2[rs, (2 * n + 1) * blk:2 * (n + 1) * blk] + bx_ref[:, cs])
            neg_lam = -lam_ref[:, cs]
            softplus = jnp.maximum(neg_lam, 0.0) + jnp.log(1.0 + jnp.exp(-jnp.abs(neg_lam)))
            log_a = -RGLRU_C * r * softplus
            a = jnp.exp(log_a)
            y = 1.0 - jnp.exp(2.0 * log_a)
            root = jnp.where(y > 0.0, y * lax.rsqrt(y), 0.0)
            u = root * (i * xc)
            hs = _recurrence(a, u, hcar[:, cs], step)
            hcar[:, cs] = hs[piece - step:piece, :]
            if rc == h_off // piece:
                @pl.when(t == h_tile)
                def _():
                    hlast_ref[:, cs] = hs[h_off % piece:h_off % piece + step, :]

            gated[rs, cs] = (hs * _gelu(gbuf[rs, cs])).astype(BF16)
        return run

    def gate_cols(ref, c0, c):
        def run():
            ref[:, c:c + MXU_WIDTH] = proj(c0 + c, c0 + c + MXU_WIDTH)
        return run

    def latent_and_key():
        o = 4 * d
        cqn[...] = _rms(proj(o, o + q_rank), qn_ref[...]).astype(BF16)
        o += q_rank
        c_kv = _rms(proj(o, o + kv_rank), kvn_ref[...])
        o += kv_rank
        ckv_ref[...] = c_kv
        ckvb_ref[...] = c_kv.astype(BF16)
        kk = proj(o, o + 2 * rope)
        k_rot = (kk * cos + pltpu.roll(kk, rope, 1) * sin)[:, :rope]
        kr_ref[...] = k_rot
        krb_ref[...] = k_rot.astype(BF16)

    hn = heads * nope
    hr = heads * rope

    def q_rope_cols(c):
        def run():
            q = cqn[...]
            qr = _dot(q, wuq_ref[:, hn + c:hn + c + MXU_WIDTH])
            qp = _dot(q, wuq_ref[:, hn + hr + c:hn + hr + c + MXU_WIDTH])
            for j in range(0, MXU_WIDTH, LANES):
                qrope_ref[:, c + j:c + j + LANES] = (qr[:, j:j + LANES] * cos + qp[:, j:j + LANES] * sin).astype(BF16)
        return run

    def q_nope_cols(c):
        def run():
            qnope[:, c:c + MXU_WIDTH] = _dot(cqn[...], wuq_ref[:, c:c + MXU_WIDTH]).astype(BF16)
        return run

    def q_lat_head(hd):
        def run():
            qlat_ref[:, hd * kv_rank:(hd + 1) * kv_rank] = _dot(qnope[:, hd * nope:(hd + 1) * nope],
                                                                wuk_ref[hd]).astype(BF16)
        return run

    others = [latent_and_key]
    others += [gate_cols(ga_ref, 2 * d, c) for c in range(0, d, MXU_WIDTH)]
    others += [gate_cols(gb_ref, 3 * d, c) for c in range(0, d, MXU_WIDTH)]
    others += [q_nope_cols(c) for c in range(0, hn, MXU_WIDTH)]
    others += [q_rope_cols(c) for c in range(0, hr, MXU_WIDTH)] + [q_lat_head(hd) for hd in range(heads)]
    n_blocks = d // blk
    n_pieces = tile // piece
    per_dot = MXU_WIDTH // blk
    vec = [rnn_piece(n, rc) for n in range(n_blocks) for rc in range(n_pieces)]
    out_a = None
    for k, item in enumerate(vec):
        item()
        for other in others[k * len(others) // len(vec):(k + 1) * len(others) // len(vec)]:
            other()
        if (k + 1) % (n_pieces * per_dot) == 0:
            n_done = (k + 1) // n_pieces
            ks = slice((n_done - per_dot) * blk, n_done * blk)
            part = _dot(gated[:, ks], wbr_ref[ks, :])
            out_a = part if out_a is None else out_a + part
    outa_ref[...] = out_a

    @pl.when(t == x_tile)
    def _():
        xlast_ref[...] = xbuf[x_off:x_off + (kconv - 1) * step, :]

    xbuf[0:pad, :] = xbuf[tile:tile + pad, :]


def _mix_in(x, cos4, sin4, conv_state, h0, wts, layer, tile, step, t_real, dims):
    g_, r_, d = x.shape
    q_rank, kv_rank, heads, nope, rope = dims
    width = wts["w_in"].shape[-1]
    kconv = wts["rnn_conv_w"].shape[1]
    nb, blk = wts["rnn_wg"].shape[1], wts["rnn_wg"].shape[2]
    pad = conv_state.shape[1]
    keep = (kconv - 1) * step
    assert pad % SUBLANES == 0 and pad >= keep and tile >= pad and blk == LANES and tile % SUBLANES == 0
    h_tile, h_off = _tail_rows(t_real, 1, step, tile, 0)
    x_tile, x_off = _tail_rows(t_real, kconv - 1, step, tile, pad)
    f = functools.partial(_mix_in_kernel, tile=tile, pad=pad, step=step, kconv=kconv, blk=blk, d=d,
                          q_rank=q_rank, kv_rank=kv_rank, heads=heads, nope=nope, rope=rope,
                          h_tile=h_tile, h_off=h_off, x_tile=x_tile, x_off=x_off)
    tab = pl.BlockSpec((tile, LANES), lambda g, t: (t, 0))
    sds = jax.ShapeDtypeStruct
    out_shape = (
        sds((g_, r_, d), F32), sds((g_, r_, d), F32), sds((g_, r_, d), F32),
        sds((g_, r_, kv_rank), F32), sds((g_, r_, kv_rank), BF16), sds((g_, r_, rope), F32), sds((g_, r_, rope), BF16),
        sds((g_, r_, heads * kv_rank), BF16), sds((g_, r_, heads * rope), BF16),
        sds((g_, step, d), F32), sds((g_, keep, d), F32),
    )
    out_specs = (
        _row_spec(tile, d), _row_spec(tile, d), _row_spec(tile, d),
        _row_spec(tile, kv_rank), _row_spec(tile, kv_rank), _row_spec(tile, rope), _row_spec(tile, rope),
        _row_spec(tile, heads * kv_rank), _row_spec(tile, heads * rope),
        _stream_spec(step, d), _stream_spec(keep, d),
    )
    in_specs = [
        _row_spec(tile, d), _layer_spec(layer, 1, d), _layer_spec(layer, d, width),
        _layer_spec(layer, 1, q_rank), _layer_spec(layer, 1, kv_rank),
        _layer_spec(layer, q_rank, wts["w_uq"].shape[-1]), _layer_spec(layer, heads, nope, kv_rank),
        tab, tab, _stream_spec(pad, d), _stream_spec(step, d),
        _layer_spec(layer, kconv, d), _layer_spec(layer, 1, d), _layer_spec(layer, nb, blk, 2 * blk),
        _layer_spec(layer, 1, d), _layer_spec(layer, 1, d), _layer_spec(layer, 1, d), _layer_spec(layer, d, d),
    ]
    return pl.pallas_call(
        f, grid=(g_, r_ // tile), in_specs=in_specs, out_specs=out_specs, out_shape=out_shape,
        scratch_shapes=[pltpu.VMEM((pad + tile, d), F32), pltpu.VMEM((tile, d), F32), pltpu.VMEM((tile, d), F32),
                        pltpu.VMEM((tile, 2 * d), F32), pltpu.VMEM((step, d), F32),
                        pltpu.VMEM((tile, d), BF16), pltpu.VMEM((tile, q_rank), BF16),
                        pltpu.VMEM((tile, heads * nope), BF16)],
        compiler_params=_params("parallel", "arbitrary"), name="mix_in",
    )(x, wts["norm_mix"], wts["w_in"], wts["q_norm"], wts["kv_norm"], wts["w_uq"], wts["w_uk"], cos4, sin4,
      conv_state, h0, wts["rnn_conv_w"], wts["rnn_conv_b"], wts["rnn_wg"], wts["rnn_ba"], wts["rnn_bx"],
      wts["rnn_lambda"], wts["w_br_rnn"])


def _prompt_attn_kernel(ql_ref, qr_ref, c_ref, kr_ref, o_ref, m_sc, l_sc, acc_sc, *, tq, tks, heads, kv, rope, coef):
    i = pl.program_id(1)
    m_sc[...] = jnp.full(m_sc.shape, -jnp.inf, F32)
    l_sc[...] = jnp.zeros(l_sc.shape, F32)
    acc_sc[...] = jnp.zeros(acc_sc.shape, F32)

    def step(k0, width, masked):
        c = c_ref[pl.ds(k0, width), :]
        kr = kr_ref[pl.ds(k0, width), :]
        if masked:
            visible = (lax.broadcasted_iota(jnp.int32, (tq, width), 1)
                       <= lax.broadcasted_iota(jnp.int32, (tq, width), 0))

        def scores(h):
            return _dot_nt(ql_ref[:, h * kv:(h + 1) * kv], c) + _dot_nt(qr_ref[:, h * rope:(h + 1) * rope], kr)

        pend = [scores(h) for h in range(min(HEADS_AHEAD, heads))]
        for h in range(heads):
            s = pend.pop(0)
            if h + HEADS_AHEAD < heads:
                pend.append(scores(h + HEADS_AHEAD))
            if masked:
                s = jnp.where(visible, s, -jnp.inf)
            m_prev = m_sc[h]
            m_new = jnp.maximum(m_prev, jnp.max(s, axis=1, keepdims=True))
            alpha = jnp.exp2((m_prev - m_new) * coef)
            p = jnp.exp2((s - jnp.concatenate([m_new] * (width // LANES), axis=1)) * coef)
            psum = p[:, :LANES]
            for t in range(1, width // LANES):
                psum = psum + p[:, t * LANES:(t + 1) * LANES]
            l_sc[h] = alpha * l_sc[h] + psum
            acc_sc[h] = jnp.concatenate([alpha] * (kv // LANES), axis=1) * acc_sc[h] + _dot(p.astype(BF16), c)
            m_sc[h] = m_new

    done = 0
    for width in tuple(tks) + (tq,):
        ratio = width // tq
        n_tiles = (i - done) // ratio

        def body(j, carry, width=width, ratio=ratio, done=done):
            step(pl.multiple_of((done + j * ratio) * tq, tq), width, False)
            return carry

        lax.fori_loop(0, n_tiles, body, 0)
        done = done + n_tiles * ratio
    step(pl.multiple_of(i * tq, tq), tq, True)
    for h in range(heads):
        l = jnp.sum(l_sc[h], axis=1, keepdims=True)
        o_ref[:, h * kv:(h + 1) * kv] = (acc_sc[h] / l).astype(BF16)


def _prompt_attn(qlat, qrope, ckvb, krb, heads, tq, tks, scale):
    g_, r_, kv_rank = ckvb.shape
    rope = krb.shape[-1]
    assert all(tk % tq == 0 for tk in tks) and r_ % tq == 0 and tq % LANES == 0
    f = functools.partial(_prompt_attn_kernel, tq=tq, tks=tks, heads=heads, kv=kv_rank, rope=rope,
                          coef=scale * LOG2E)
    return pl.pallas_call(
        f, grid=(g_, r_ // tq),
        in_specs=[_row_spec(tq, heads * kv_rank), _row_spec(tq, heads * rope),
                  _stream_spec(r_, kv_rank), _stream_spec(r_, rope)],
        out_specs=_row_spec(tq, heads * kv_rank),
        out_shape=jax.ShapeDtypeStruct((g_, r_, heads * kv_rank), BF16),
        scratch_shapes=[pltpu.VMEM((heads, tq, LANES), F32), pltpu.VMEM((heads, tq, LANES), F32),
                        pltpu.VMEM((heads, tq, kv_rank), F32)],
        compiler_params=_params("parallel", "parallel"), name="prompt_attn",
    )(qlat, qrope, ckvb, krb)


def _sample_attn_kernel(pt_ref, ql_ref, qr_ref, cn_ref, kn_ref, cache_c, cache_rt, o_ref,
                        cbuf, rbuf, sem_c, sem_r, *, layer, n_pages, page, heads, t_new, cpages, coef):
    b = pl.program_id(0)
    nb = pl.num_programs(0)

    def page_copies(seq, slot, p):
        pg = pt_ref[seq * n_pages + p]
        return (pltpu.make_async_copy(cache_c.at[layer, pg], cbuf.at[slot, p], sem_c.at[slot]),
                pltpu.make_async_copy(cache_rt.at[layer, pg], rbuf.at[slot, p], sem_r.at[slot]))

    def start_fetch(seq, slot):
        def body(p, carry):
            for cp in page_copies(seq, slot, p):
                cp.start()
            return carry
        lax.fori_loop(0, n_pages, body, 0, unroll=DMA_UNROLL)

    def wait_fetch(seq, slot):
        def body(p, carry):
            for cp in page_copies(seq, slot, p):
                cp.wait()
            return carry
        lax.fori_loop(0, n_pages, body, 0, unroll=DMA_UNROLL)

    slot = lax.rem(b, 2)

    @pl.when(b == 0)
    def _():
        start_fetch(b, slot)

    @pl.when(b + 1 < nb)
    def _():
        start_fetch(b + 1, 1 - slot)

    wait_fetch(b, slot)

    ql = ql_ref[...]
    qr = qr_ref[...]
    kv = ql.shape[1]
    ck = cpages * page
    n_chunks = n_pages // cpages

    def scores(j):
        c = cbuf[slot, j * cpages:(j + 1) * cpages].reshape(ck, kv).astype(BF16)
        s_rope = jnp.concatenate([_dot(qr, rbuf[slot, j * cpages + p].astype(BF16)) for p in range(cpages)], axis=1)
        return _dot_nt(ql, c) + s_rope, c

    cn = cn_ref[...].astype(BF16)
    kn = kn_ref[...].astype(BF16)
    s_new = _dot_nt(ql, cn) + _dot_nt(qr, kn)
    q_tok = lax.broadcasted_iota(jnp.int32, s_new.shape, 0) // heads
    k_tok = lax.broadcasted_iota(jnp.int32, s_new.shape, 1)
    s_new = jnp.where((k_tok <= q_tok) & (k_tok < t_new), s_new, -jnp.inf)
    m = jnp.max(s_new, axis=1, keepdims=True)
    p_new = jnp.exp2((s_new - m) * coef)
    l = jnp.sum(p_new, axis=1, keepdims=True)
    acc = _dot(p_new.astype(BF16), cn)

    pend = [scores(j) for j in range(min(CHUNKS_AHEAD, n_chunks))]
    for j in range(n_chunks):
        s, c = pend.pop(0)
        if j + CHUNKS_AHEAD < n_chunks:
            pend.append(scores(j + CHUNKS_AHEAD))
        m_new = jnp.maximum(m, jnp.max(s, axis=1, keepdims=True))
        alpha = jnp.exp2((m - m_new) * coef)
        p = jnp.exp2((s - m_new) * coef)
        l = alpha * l + jnp.sum(p, axis=1, keepdims=True)
        acc = alpha * acc + _dot(p.astype(BF16), c)
        m = m_new
    o_ref[...] = (acc / l).astype(BF16)


def _sample_attn(page_table, qlat, qrope, c_new, k_new, cache_c, cache_rt, layer, heads, scale):
    bs, rows, kv_rank = qlat.shape
    rope = qrope.shape[-1]
    t_pad = c_new.shape[1]
    n_pages = page_table.shape[1]
    page = cache_c.shape[2]
    cpages = math.gcd(PAGES_PER_CHUNK, n_pages)
    f = functools.partial(_sample_attn_kernel, layer=layer, n_pages=n_pages, page=page, heads=heads,
                          t_new=rows // heads, cpages=cpages, coef=scale * LOG2E)
    seq = lambda r, c: pl.BlockSpec((None, r, c), lambda b, pt: (b, 0, 0))
    grid_spec = pltpu.PrefetchScalarGridSpec(
        num_scalar_prefetch=1, grid=(bs,),
        in_specs=[seq(rows, kv_rank), seq(rows, rope), seq(t_pad, kv_rank), seq(t_pad, rope),
                  pl.BlockSpec(memory_space=pl.ANY), pl.BlockSpec(memory_space=pl.ANY)],
        out_specs=seq(rows, kv_rank),
        scratch_shapes=[pltpu.VMEM((2, n_pages, page, kv_rank), F32), pltpu.VMEM((2, n_pages, rope, page), F32),
                        pltpu.SemaphoreType.DMA((2,)), pltpu.SemaphoreType.DMA((2,))],
    )
    return pl.pallas_call(
        f, grid_spec=grid_spec, out_shape=jax.ShapeDtypeStruct((bs, rows, kv_rank), BF16),
        compiler_params=_params("arbitrary"), name="sample_attn",
    )(page_table.reshape(-1), qlat, qrope, c_new, k_new, cache_c, cache_rt)


def _mix_out_kernel(o_ref, a_ref, ga_ref, gb_ref, x_ref, wuv_ref, wbr_ref, wout_ref,
                    nf_ref, st_ref, wup_ref, cw_ref, cb_ref, wdn_ref, nfin_ref,
                    x2_ref, ulast_ref, y_ref, ov, ubuf,
                    *, heads, kv_rank, v_dim, tile, pad, step, kconv, ff, fc, u_tile, u_off):
    t = pl.program_id(1)

    @pl.when(t == 0)
    def _():
        ubuf[0:pad, :] = st_ref[...]

    for hd in range(heads):
        ov[:, hd * v_dim:(hd + 1) * v_dim] = _dot(o_ref[:, hd * kv_rank:(hd + 1) * kv_rank], wuv_ref[hd]).astype(BF16)
    out_b = _dot(ov[...], wbr_ref[...])
    merged = _sigmoid(ga_ref[...]) * a_ref[...] + _sigmoid(gb_ref[...]) * out_b
    x1 = x_ref[...] + _dot(merged.astype(BF16), wout_ref[...])

    hf = _rms(x1, nf_ref[...]).astype(BF16)
    acc = x1

    def up(c):
        return _dot(hf, wup_ref[:, c * fc:(c + 1) * fc]), _dot(hf, wup_ref[:, ff + c * fc:ff + (c + 1) * fc])

    n_chunks = ff // fc
    pend = [up(c) for c in range(min(FFN_AHEAD, n_chunks))]
    for c in range(n_chunks):
        cs = slice(c * fc, (c + 1) * fc)
        u, v = pend.pop(0)
        if c + FFN_AHEAD < n_chunks:
            pend.append(up(c + FFN_AHEAD))
        ubuf[pad:pad + tile, cs] = u
        uc = cb_ref[:, cs] + cw_ref[kconv - 1:kconv, cs] * u
        for k in range(kconv - 1):
            off = pad - (kconv - 1 - k) * step
            uc = uc + cw_ref[k:k + 1, cs] * ubuf[off:off + tile, cs]
        acc = acc + _dot((_gelu(uc) * v).astype(BF16), wdn_ref[cs, :])
    x2_ref[...] = acc
    if y_ref is not None:
        y_ref[...] = _rms(acc, nfin_ref[...])

    @pl.when(t == u_tile)
    def _():
        ulast_ref[...] = ubuf[u_off:u_off + (kconv - 1) * step, :]

    ubuf[0:pad, :] = ubuf[tile:tile + pad, :]


def _mix_out(o_lat, out_a, g_a, g_b, x, conv_state, norm_final, wts, layer, tile, step, t_real, final):
    g_, r_, d = x.shape
    heads, kv_rank, v_dim = wts["w_uv"].shape[1:]
    kconv, ff = wts["ffn_conv_w"].shape[1:]
    pad = conv_state.shape[1]
    keep = (kconv - 1) * step
    fc = math.gcd(FFN_CHUNK, ff)
    assert pad % SUBLANES == 0 and pad >= keep and tile >= pad
    u_tile, u_off = _tail_rows(t_real, kconv - 1, step, tile, pad)
    body = functools.partial(_mix_out_kernel, heads=heads, kv_rank=kv_rank, v_dim=v_dim, tile=tile, pad=pad,
                             step=step, kconv=kconv, ff=ff, fc=fc, u_tile=u_tile, u_off=u_off)
    n_in = 15
    if final:
        f = body
    else:
        def f(*refs):
            return body(*refs[:n_in + 2], None, *refs[n_in + 2:])
    sds = jax.ShapeDtypeStruct
    out_specs = [_row_spec(tile, d), _stream_spec(keep, ff)]
    out_shape = [sds((g_, r_, d), F32), sds((g_, keep, ff), F32)]
    if final:
        out_specs.append(_row_spec(tile, d))
        out_shape.append(sds((g_, r_, d), F32))
    return pl.pallas_call(
        f, grid=(g_, r_ // tile),
        in_specs=[_row_spec(tile, heads * kv_rank), _row_spec(tile, d), _row_spec(tile, d), _row_spec(tile, d),
                  _row_spec(tile, d), _layer_spec(layer, heads, kv_rank, v_dim),
                  _layer_spec(layer, heads * v_dim, d), _layer_spec(layer, d, d),
                  _layer_spec(layer, 1, d), _stream_spec(pad, ff),
                  _layer_spec(layer, d, 2 * ff), _layer_spec(layer, kconv, ff), _layer_spec(layer, 1, ff),
                  _layer_spec(layer, ff, d), pl.BlockSpec((1, d), lambda g, t: (0, 0))],
        out_specs=tuple(out_specs), out_shape=tuple(out_shape),
        scratch_shapes=[pltpu.VMEM((tile, heads * v_dim), BF16), pltpu.VMEM((pad + tile, ff), F32)],
        compiler_params=_params("parallel", "arbitrary"), name="mix_out",
    )(o_lat, out_a, g_a, g_b, x, wts["w_uv"], wts["w_br_attn"], wts["w_out"], wts["norm_ffn"], conv_state,
      wts["ffn_w_up"], wts["ffn_conv_w"], wts["ffn_conv_b"], wts["ffn_w_down"], norm_final)


def _rope_tables(pos, half):
    inv = ROPE_THETA ** (-jnp.arange(half, dtype=F32) / half)
    ang = pos.astype(F32)[:, None] * inv[None, :]
    c, s = jnp.cos(ang), jnp.sin(ang)
    return jnp.concatenate([c, c, c, c], axis=1), jnp.concatenate([-s, s, -s, s], axis=1)


def _swap_halves(w):
    half = w.shape[-1] // 2
    return jnp.concatenate([w[..., half:], w[..., :half]], axis=-1)


def _pack_weights(p, d, q_rank, kv_rank, heads, nope, rope):
    depth = p["w_in"].shape[0]
    w_in = p["w_in"]
    o_cq = 2 * d
    o_kr = o_cq + q_rank + kv_rank
    o_ga = o_kr + rope
    k_r = w_in[:, :, o_kr:o_ga]
    w_in_p = jnp.concatenate([w_in[:, :, :2 * d], w_in[:, :, o_ga:o_ga + 2 * d], w_in[:, :, o_cq:o_kr],
                              k_r, _swap_halves(k_r)], axis=-1).astype(BF16)
    w_uq = p["w_uq"].reshape(depth, q_rank, heads, nope + rope)
    q_r = w_uq[..., nope:]
    w_uq_p = jnp.concatenate([w_uq[..., :nope].reshape(depth, q_rank, heads * nope),
                              q_r.reshape(depth, q_rank, heads * rope),
                              _swap_halves(q_r).reshape(depth, q_rank, heads * rope)], axis=-1).astype(BF16)
    row = lambda a: a[:, None, :]
    return {
        "norm_mix": row(p["norm_mix"]), "w_in": w_in_p, "q_norm": row(p["q_norm"]), "kv_norm": row(p["kv_norm"]),
        "w_uq": w_uq_p, "w_uk": p["w_uk"].transpose(0, 2, 3, 1).astype(BF16),
        "w_uv": p["w_uv"].transpose(0, 2, 1, 3).astype(BF16),
        "rnn_conv_w": p["rnn_conv_w"], "rnn_conv_b": row(p["rnn_conv_b"]),
        "rnn_wg": jnp.concatenate([p["rnn_wa"], p["rnn_wx"]], axis=-1).astype(BF16),
        "rnn_ba": row(p["rnn_ba"]), "rnn_bx": row(p["rnn_bx"]), "rnn_lambda": row(p["rnn_lambda"]),
        "w_br_rnn": p["w_br_rnn"].astype(BF16), "w_br_attn": p["w_br_attn"].astype(BF16),
        "w_out": p["w_out"].astype(BF16), "norm_ffn": row(p["norm_ffn"]),
        "ffn_w_up": p["ffn_w_up"].astype(BF16), "ffn_conv_w": p["ffn_conv_w"],
        "ffn_conv_b": row(p["ffn_conv_b"]), "ffn_w_down": p["ffn_w_down"].astype(BF16),
    }


def _pad_rows(a, rows):
    return jnp.pad(a, ((0, 0), (rows - a.shape[1], 0), (0, 0)))


def kernel(x_prompt, x_sample, cache_kv_latent, cache_k_rope, state_rnn_h, state_rnn_conv, state_ffn_conv,
           page_table, meta_tokens, norm_mix, w_in, rnn_conv_w, rnn_conv_b, rnn_wa, rnn_ba, rnn_wx, rnn_bx,
           rnn_lambda, q_norm, w_uq, kv_norm, w_uk, w_uv, w_br_rnn, w_br_attn, w_out, norm_ffn, ffn_w_up,
           ffn_conv_w, ffn_conv_b, ffn_w_down, norm_final):
    bp, seq, d = x_prompt.shape
    bs, ts, _ = x_sample.shape
    depth = w_in.shape[0]
    n_meta = meta_tokens.shape[0]
    kv_rank, heads, nope = w_uk.shape[1:]
    q_rank = w_uq.shape[1]
    rope = w_uq.shape[2] // heads - nope
    ff = ffn_w_down.shape[1]
    rnn_k = rnn_conv_w.shape[1]
    ffn_k = ffn_conv_w.shape[1]
    past_len = page_table.shape[1] * cache_kv_latent.shape[2]
    scale = 1.0 / math.sqrt(nope + rope)
    assert 2 * rope == LANES and kv_rank % LANES == 0 and nope % LANES == 0 and d % MXU_WIDTH == 0
    dims = (q_rank, kv_rank, heads, nope, rope)

    wts = _pack_weights(
        dict(norm_mix=norm_mix, w_in=w_in, rnn_conv_w=rnn_conv_w, rnn_conv_b=rnn_conv_b, rnn_wa=rnn_wa,
             rnn_ba=rnn_ba, rnn_wx=rnn_wx, rnn_bx=rnn_bx, rnn_lambda=rnn_lambda, q_norm=q_norm, w_uq=w_uq,
             kv_norm=kv_norm, w_uk=w_uk, w_uv=w_uv, w_br_rnn=w_br_rnn, w_br_attn=w_br_attn, w_out=w_out,
             norm_ffn=norm_ffn, ffn_w_up=ffn_w_up, ffn_conv_w=ffn_conv_w, ffn_conv_b=ffn_conv_b,
             ffn_w_down=ffn_w_down),
        d, q_rank, kv_rank, heads, nope, rope)
    nfin = norm_final[None]

    lp = seq + n_meta
    lpad = _round_up(lp, PROMPT_TILE)
    xp = jnp.concatenate([jnp.broadcast_to(meta_tokens[None].astype(x_prompt.dtype), (bp, n_meta, d)), x_prompt,
                          jnp.zeros((bp, lpad - lp, d), x_prompt.dtype)], axis=1)
    cos_p, sin_p = _rope_tables(jnp.arange(lpad, dtype=jnp.int32), rope // 2)
    zeros_p = lambda c: jnp.zeros((bp, SUBLANES, c), F32)
    rs = ts * bs
    xs = x_sample.transpose(1, 0, 2).reshape(1, rs, d)
    cos_s, sin_s = _rope_tables(past_len + jnp.repeat(jnp.arange(ts, dtype=jnp.int32), bs), rope // 2)
    t_pad = _round_up(ts, 2 * SUBLANES)
    cache_rt = cache_k_rope.transpose(0, 1, 3, 2)

    def time_major(state):
        return state.transpose(1, 0, 2).reshape(1, state.shape[1] * bs, state.shape[2])

    def batch_major(a, k):
        return a.reshape(k, bs, a.shape[-1]).transpose(1, 0, 2)

    outs = {k: [] for k in ("p_lat", "p_rope", "p_h", "p_rc", "p_fc", "s_lat", "s_rope", "s_h", "s_rc", "s_fc")}
    y_p = y_s = None
    for l in range(depth):
        final = l == depth - 1
        out_a, g_a, g_b, ckv, ckvb, kr, krb, qlat, qrope, h_last, x_last = _mix_in(
            xp, cos_p, sin_p, zeros_p(d), jnp.zeros((bp, 1, d), F32), wts, l, PROMPT_TILE, 1, lp, dims)
        o_lat = _prompt_attn(qlat, qrope, ckvb, krb, heads, PROMPT_TILE, ATTN_KEY_TILES, scale)
        res = _mix_out(o_lat, out_a, g_a, g_b, xp, zeros_p(ff), nfin, wts, l, PROMPT_TILE, 1, lp, final)
        xp, u_last = res[0], res[1]
        if final:
            y_p = res[2]
        outs["p_lat"].append(ckv[:, :lp])
        outs["p_rope"].append(kr[:, :lp])
        outs["p_h"].append(h_last[:, 0])
        outs["p_rc"].append(x_last)
        outs["p_fc"].append(u_last)

        rc0 = time_major(state_rnn_conv[l])
        out_a, g_a, g_b, ckv, ckvb, kr, krb, qlat, qrope, h_last, x_last = _mix_in(
            xs, cos_s, sin_s, _pad_rows(rc0, _round_up(rc0.shape[1], SUBLANES)), state_rnn_h[l][None],
            wts, l, rs, bs, ts, dims)
        to_seq = lambda a: jnp.pad(batch_major(a, ts), ((0, 0), (0, t_pad - ts), (0, 0)))
        o_seq = _sample_attn(page_table, batch_major(qlat, ts).reshape(bs, ts * heads, kv_rank),
                             batch_major(qrope, ts).reshape(bs, ts * heads, rope),
                             to_seq(ckv), to_seq(kr), cache_kv_latent, cache_rt, l, heads, scale)
        o_lat = o_seq.reshape(bs, ts, heads * kv_rank).transpose(1, 0, 2).reshape(1, rs, heads * kv_rank)
        fc0 = time_major(state_ffn_conv[l])
        res = _mix_out(o_lat, out_a, g_a, g_b, xs, _pad_rows(fc0, _round_up(fc0.shape[1], SUBLANES)), nfin,
                       wts, l, rs, bs, ts, final)
        xs, u_last = res[0], res[1]
        if final:
            y_s = res[2]
        outs["s_lat"].append(batch_major(ckv, ts))
        outs["s_rope"].append(batch_major(kr, ts))
        outs["s_h"].append(h_last[0])
        outs["s_rc"].append(batch_major(x_last, rnn_k - 1))
        outs["s_fc"].append(batch_major(u_last, ffn_k - 1))

    y_prompt = y_p[:, n_meta:lp]
    y_sample = batch_major(y_s, ts)
    st = {k: jnp.stack(v) for k, v in outs.items()}
    return (y_prompt, y_sample, st["p_lat"], st["p_rope"], st["p_h"], st["p_rc"], st["p_fc"],
            st["s_lat"], st["s_rope"], st["s_h"], st["s_rc"], st["s_fc"])
```

```python
import functools
import math

import jax
import jax.numpy as jnp
from jax import lax
from jax.experimental import pallas as pl
from jax.experimental.pallas import tpu as pltpu

F32 = jnp.float32
BF16 = jnp.bfloat16

EPS = 1e-6
RGLRU_C = 8.0
ROPE_THETA = 10000.0
LANES = 128
SUBLANES = 8
MXU_WIDTH = 256
VMEM_LIMIT = 56 * 1024 * 1024
LOG2E = 1.4426950408889634
PROMPT_TILE = 256
ATTN_KEY_TILES = (1024, 512)
HEADS_AHEAD = 3
PAGES_PER_CHUNK = 8
CHUNKS_AHEAD = 2
DMA_UNROLL = 8
FFN_CHUNK = 512
RNN_PIECE = 32
FFN_AHEAD = 2


def _round_up(x, m):
    return (x + m - 1) // m * m


def _rms(x, g):
    return x * lax.rsqrt(jnp.mean(x * x, axis=-1, keepdims=True) + EPS) * g


def _gelu(x):
    return 0.5 * x * (1.0 + jnp.tanh(math.sqrt(2.0 / math.pi) * (x + 0.044715 * (x * x * x))))


def _sigmoid(x):
    return 1.0 / (1.0 + jnp.exp(-x))


def _dot(a, b):
    return jnp.dot(a, b, preferred_element_type=F32)


def _dot_nt(a, b):
    return lax.dot_general(a, b, (((1,), (1,)), ((), ())), preferred_element_type=F32)


def _params(*sem):
    return pltpu.CompilerParams(dimension_semantics=sem, vmem_limit_bytes=VMEM_LIMIT)


def _row_spec(tile, cols):
    return pl.BlockSpec((None, tile, cols), lambda g, t: (g, t, 0))


def _stream_spec(rows, cols):
    return pl.BlockSpec((None, rows, cols), lambda g, t: (g, 0, 0))


def _layer_spec(layer, *shape):
    zeros = (0,) * len(shape)
    return pl.BlockSpec((None,) + tuple(shape), lambda g, t: (layer,) + zeros)


def _tail_rows(t_real, keep, step, tile, pad):
    first_row = (t_real - keep) * step
    last_tile = (first_row + keep * step - 1) // tile
    off = pad + first_row - last_tile * tile
    assert off >= 0
    return last_tile, off


def _scan_rows(a, u, step):
    n = a.shape[0]
    row = lax.broadcasted_iota(jnp.int32, a.shape, 0)
    d = step
    while d < n:
        keep = row >= d
        a_prev = jnp.where(keep, pltpu.roll(a, d, 0), 1.0)
        u_prev = jnp.where(keep, pltpu.roll(u, d, 0), 0.0)
        u = a * u_prev + u
        a = a * a_prev
        d *= 2
    return a, u


def _recurrence(a, u, h_in, step):
    n, width = a.shape
    if step > 1:
        a_p, u_p = _scan_rows(a, u, step)
        return a_p * jnp.concatenate([h_in] * (n // step), axis=0) + u_p
    groups = n // SUBLANES
    a3 = a.reshape(groups, SUBLANES, width)
    u3 = u.reshape(groups, SUBLANES, width)
    sub = lax.broadcasted_iota(jnp.int32, a3.shape, 1)
    d = 1
    while d < SUBLANES:
        keep = sub >= d
        a_prev = jnp.where(keep, pltpu.roll(a3, d, 1), 1.0)
        u_prev = jnp.where(keep, pltpu.roll(u3, d, 1), 0.0)
        u3 = a3 * u_prev + u3
        a3 = a3 * a_prev
        d *= 2
    carry = h_in
    out = []
    for g in range(groups):
        h = a3[g] * carry + u3[g]
        out.append(h)
        carry = h[SUBLANES - 1:SUBLANES, :]
    return jnp.concatenate(out, axis=0)


def _mix_in_kernel(x_ref, nm_ref, w_ref, qn_ref, kvn_ref, wuq_ref, wuk_ref, cos_ref, sin_ref,
                   st_ref, h0_ref, cw_ref, cb_ref, wg_ref, ba_ref, bx_ref, lam_ref, wbr_ref,
                   outa_ref, ga_ref, gb_ref, ckv_ref, ckvb_ref, kr_ref, krb_ref, qlat_ref, qrope_ref,
                   hlast_ref, xlast_ref, xbuf, gbuf, xcbuf, gbuf2, hcar, gated, cqn, qnope,
                   *, tile, pad, step, kconv, blk, d, q_rank, kv_rank, heads, nope, rope,
                   h_tile, h_off, x_tile, x_off):
    t = pl.program_id(1)

    @pl.when(t == 0)
    def _():
        xbuf[0:pad, :] = st_ref[...]
        hcar[...] = h0_ref[...]

    h = _rms(x_ref[...], nm_ref[...]).astype(BF16)

    def proj(c0, c1):
        return _dot(h, w_ref[:, c0:c1])

    xbuf[pad:pad + tile, :] = proj(0, d)
    gbuf[...] = proj(d, 2 * d)
    cos = cos_ref[...]
    sin = sin_ref[...]

    for n in range(d // blk):
        cs = slice(n * blk, (n + 1) * blk)
        xc = cb_ref[:, cs] + cw_ref[kconv - 1:kconv, cs] * xbuf[pad:pad + tile, cs]
        for k in range(kconv - 1):
            off = pad - (kconv - 1 - k) * step
            xc = xc + cw_ref[k:k + 1, cs] * xbuf[off:off + tile, cs]
        xcbuf[:, cs] = xc
        gbuf2[:, 2 * n * blk:2 * (n + 1) * blk] = _dot(xc.astype(BF16), wg_ref[n])

    piece = RNN_PIECE if step == 1 else tile

    def rnn_piece(n, rc):
        def run():
            cs = slice(n * blk, (n + 1) * blk)
            rs = slice(rc * piece, (rc + 1) * piece)
            xc = xcbuf[rs, cs]
            r = _sigmoid(gbuf2[rs, 2 * n * blk:(2 * n + 1) * blk] + ba_ref[:, cs])
            i = _sigmoid(gbuf2[rs, (2 * n + 1) * blk:2 * (n + 1) * blk] + bx_ref[:, cs])
            neg_lam = -lam_ref[:, cs]
            softplus = jnp.maximum(neg_lam, 0.0) + jnp.log(1.0 + jnp.exp(-jnp.abs(neg_lam)))
            log_a = -RGLRU_C * r * softplus
            a = jnp.exp(log_a)
            y = 1.0 - jnp.exp(2.0 * log_a)
            root = jnp.where(y > 0.0, y * lax.rsqrt(y), 0.0)
            u = root * (i * xc)
            hs = _recurrence(a, u, hcar[:, cs], step)
            hcar[:, cs] = hs[piece - step:piece, :]
            if rc == h_off // piece:
                @pl.when(t == h_tile)
                def _():
                    hlast_ref[:, cs] = hs[h_off % piece:h_off % piece + step, :]

            gated[rs, cs] = (hs * _gelu(gbuf[rs, cs])).astype(BF16)
        return run

    def gate_cols(ref, c0, c):
        def run():
            ref[:, c:c + MXU_WIDTH] = proj(c0 + c, c0 + c + MXU_WIDTH)
        return run

    def latent_and_key():
        o = 4 * d
        cqn[...] = _rms(proj(o, o + q_rank), qn_ref[...]).astype(BF16)
        o += q_rank
        c_kv = _rms(proj(o, o + kv_rank), kvn_ref[...])
        o += kv_rank
        ckv_ref[...] = c_kv
        ckvb_ref[...] = c_kv.astype(BF16)
        kk = proj(o, o + 2 * rope)
        k_rot = (kk * cos + pltpu.roll(kk, rope, 1) * sin)[:, :rope]
        kr_ref[...] = k_rot
        krb_ref[...] = k_rot.astype(BF16)

    hn = heads * nope
    hr = heads * rope

    def q_rope_cols(c):
        def run():
            q = cqn[...]
            qr = _dot(q, wuq_ref[:, hn + c:hn + c + MXU_WIDTH])
            qp = _dot(q, wuq_ref[:, hn + hr + c:hn + hr + c + MXU_WIDTH])
            for j in range(0, MXU_WIDTH, LANES):
                qrope_ref[:, c + j:c + j + LANES] = (qr[:, j:j + LANES] * cos + qp[:, j:j + LANES] * sin).astype(BF16)
        return run

    def q_nope_cols(c):
        def run():
            qnope[:, c:c + MXU_WIDTH] = _dot(cqn[...], wuq_ref[:, c:c + MXU_WIDTH]).astype(BF16)
        return run

    def q_lat_head(hd):
        def run():
            qlat_ref[:, hd * kv_rank:(hd + 1) * kv_rank] = _dot(qnope[:, hd * nope:(hd + 1) * nope],
                                                                wuk_ref[hd]).astype(BF16)
        return run

    others = [latent_and_key]
    others += [gate_cols(ga_ref, 2 * d, c) for c in range(0, d, MXU_WIDTH)]
    others += [gate_cols(gb_ref, 3 * d, c) for c in range(0, d, MXU_WIDTH)]
    others += [q_nope_cols(c) for c in range(0, hn, MXU_WIDTH)]
    others += [q_rope_cols(c) for c in range(0, hr, MXU_WIDTH)] + [q_lat_head(hd) for hd in range(heads)]
    n_blocks = d // blk
    n_pieces = tile // piece
    per_dot = MXU_WIDTH // blk
    vec = [rnn_piece(n, rc) for n in range(n_blocks) for rc in range(n_pieces)]
    out_a = None
    for k, item in enumerate(vec):
        item()
        for other in others[k * len(others) // len(vec):(k + 1) * len(others) // len(vec)]:
            other()
        if (k + 1) % (n_pieces * per_dot) == 0:
            n_done = (k + 1) // n_pieces
            ks = slice((n_done - per_dot) * blk, n_done * blk)
            part = _dot(gated[:, ks], wbr_ref[ks, :])
            out_a = part if out_a is None else out_a + part
    outa_ref[...] = out_a

    @pl.when(t == x_tile)
    def _():
        xlast_ref[...] = xbuf[x_off:x_off + (kconv - 1) * step, :]

    xbuf[0:pad, :] = xbuf[tile:tile + pad, :]


def _mix_in(x, cos4, sin4, conv_state, h0, wts, layer, tile, step, t_real, dims):
    g_, r_, d = x.shape
    q_rank, kv_rank, heads, nope, rope = dims
    width = wts["w_in"].shape[-1]
    kconv = wts["rnn_conv_w"].shape[1]
    nb, blk = wts["rnn_wg"].shape[1], wts["rnn_wg"].shape[2]
    pad = conv_state.shape[1]
    keep = (kconv - 1) * step
    assert pad % SUBLANES == 0 and pad >= keep and tile >= pad and blk == LANES and tile % SUBLANES == 0
    h_tile, h_off = _tail_rows(t_real, 1, step, tile, 0)
    x_tile, x_off = _tail_rows(t_real, kconv - 1, step, tile, pad)
    f = functools.partial(_mix_in_kernel, tile=tile, pad=pad, step=step, kconv=kconv, blk=blk, d=d,
                          q_rank=q_rank, kv_rank=kv_rank, heads=heads, nope=nope, rope=rope,
                          h_tile=h_tile, h_off=h_off, x_tile=x_tile, x_off=x_off)
    tab = pl.BlockSpec((tile, LANES), lambda g, t: (t, 0))
    sds = jax.ShapeDtypeStruct
    out_shape = (
        sds((g_, r_, d), F32), sds((g_, r_, d), F32), sds((g_, r_, d), F32),
        sds((g_, r_, kv_rank), F32), sds((g_, r_, kv_rank), BF16), sds((g_, r_, rope), F32), sds((g_, r_, rope), BF16),
        sds((g_, r_, heads * kv_rank), BF16), sds((g_, r_, heads * rope), BF16),
        sds((g_, step, d), F32), sds((g_, keep, d), F32),
    )
    out_specs = (
        _row_spec(tile, d), _row_spec(tile, d), _row_spec(tile, d),
        _row_spec(tile, kv_rank), _row_spec(tile, kv_rank), _row_spec(tile, rope), _row_spec(tile, rope),
        _row_spec(tile, heads * kv_rank), _row_spec(tile, heads * rope),
        _stream_spec(step, d), _stream_spec(keep, d),
    )
    in_specs = [
        _row_spec(tile, d), _layer_spec(layer, 1, d), _layer_spec(layer, d, width),
        _layer_spec(layer, 1, q_rank), _layer_spec(layer, 1, kv_rank),
        _layer_spec(layer, q_rank, wts["w_uq"].shape[-1]), _layer_spec(layer, heads, nope, kv_rank),
        tab, tab, _stream_spec(pad, d), _stream_spec(step, d),
        _layer_spec(layer, kconv, d), _layer_spec(layer, 1, d), _layer_spec(layer, nb, blk, 2 * blk),
        _layer_spec(layer, 1, d), _layer_spec(layer, 1, d), _layer_spec(layer, 1, d), _layer_spec(layer, d, d),
    ]
    return pl.pallas_call(
        f, grid=(g_, r_ // tile), in_specs=in_specs, out_specs=out_specs, out_shape=out_shape,
        scratch_shapes=[pltpu.VMEM((pad + tile, d), F32), pltpu.VMEM((tile, d), F32), pltpu.VMEM((tile, d), F32),
                        pltpu.VMEM((tile, 2 * d), F32), pltpu.VMEM((step, d), F32),
                        pltpu.VMEM((tile, d), BF16), pltpu.VMEM((tile, q_rank), BF16),
                        pltpu.VMEM((tile, heads * nope), BF16)],
        compiler_params=_params("parallel", "arbitrary"), name="mix_in",
    )(x, wts["norm_mix"], wts["w_in"], wts["q_norm"], wts["kv_norm"], wts["w_uq"], wts["w_uk"], cos4, sin4,
      conv_state, h0, wts["rnn_conv_w"], wts["rnn_conv_b"], wts["rnn_wg"], wts["rnn_ba"], wts["rnn_bx"],
      wts["rnn_lambda"], wts["w_br_rnn"])


def _prompt_attn_kernel(ql_ref, qr_ref, c_ref, kr_ref, o_ref, m_sc, l_sc, acc_sc,
                        *, tq, tks, heads, kv, rope, coef, n_last):
    i = pl.program_id(1)

    def attend(chains):
        for idx, rows, _, _, _ in chains:
            m_sc[idx, 0:rows] = jnp.full((rows, LANES), -jnp.inf, F32)
            l_sc[idx, 0:rows] = jnp.zeros((rows, LANES), F32)
            acc_sc[idx, 0:rows] = jnp.zeros((rows, kv), F32)

        def step(k0, width, masked):
            c = c_ref[pl.ds(k0, width), :]
            kr = kr_ref[pl.ds(k0, width), :]

            def scores(n):
                return _dot_nt(chains[n][2](), c) + _dot_nt(chains[n][3](), kr)

            pend = [scores(n) for n in range(min(HEADS_AHEAD, len(chains)))]
            for n, (idx, rows, _, _, token) in enumerate(chains):
                s = pend.pop(0)
                if n + HEADS_AHEAD < len(chains):
                    pend.append(scores(n + HEADS_AHEAD))
                if masked:
                    s = jnp.where(lax.broadcasted_iota(jnp.int32, (rows, width), 1) <= token(width), s, -jnp.inf)
                m_prev = m_sc[idx, 0:rows]
                m_new = jnp.maximum(m_prev, jnp.max(s, axis=1, keepdims=True))
                alpha = jnp.exp2((m_prev - m_new) * coef)
                p = jnp.exp2((s - jnp.concatenate([m_new] * (width // LANES), axis=1)) * coef)
                psum = p[:, :LANES]
                for t in range(1, width // LANES):
                    psum = psum + p[:, t * LANES:(t + 1) * LANES]
                l_sc[idx, 0:rows] = alpha * l_sc[idx, 0:rows] + psum
                acc_sc[idx, 0:rows] = (jnp.concatenate([alpha] * (kv // LANES), axis=1) * acc_sc[idx, 0:rows]
                                       + _dot(p.astype(BF16), c))
                m_sc[idx, 0:rows] = m_new

        done = 0
        for width in tuple(tks) + (tq,):
            ratio = width // tq
            n_tiles = (i - done) // ratio

            def body(j, carry, width=width, ratio=ratio, done=done):
                step(pl.multiple_of((done + j * ratio) * tq, tq), width, False)
                return carry

            lax.fori_loop(0, n_tiles, body, 0)
            done = done + n_tiles * ratio
        step(pl.multiple_of(i * tq, tq), tq, True)

    def normalised(idx, rows):
        return acc_sc[idx, 0:rows] / jnp.sum(l_sc[idx, 0:rows], axis=1, keepdims=True)

    def all_rows():
        attend([(h, tq, lambda h=h: ql_ref[:, h * kv:(h + 1) * kv], lambda h=h: qr_ref[:, h * rope:(h + 1) * rope],
                 lambda width: lax.broadcasted_iota(jnp.int32, (tq, width), 0)) for h in range(heads)])
        for h in range(heads):
            o_ref[:, h * kv:(h + 1) * kv] = normalised(h, tq).astype(BF16)

    def first_rows():
        rows = heads * n_last
        ql = jnp.concatenate([ql_ref[0:n_last, h * kv:(h + 1) * kv] for h in range(heads)], axis=0)
        qr = jnp.concatenate([qr_ref[0:n_last, h * rope:(h + 1) * rope] for h in range(heads)], axis=0)
        token = lambda width: jnp.concatenate([lax.broadcasted_iota(jnp.int32, (n_last, width), 0)] * heads, axis=0)
        attend([(0, rows, lambda: ql, lambda: qr, token)])
        o = normalised(0, rows).astype(BF16)
        o_ref[...] = jnp.zeros(o_ref.shape, BF16)
        for h in range(heads):
            o_ref[0:n_last, h * kv:(h + 1) * kv] = o[h * n_last:(h + 1) * n_last, :]

    if n_last == tq or heads * n_last > tq:
        all_rows()
    else:
        last = pl.num_programs(1) - 1
        pl.when(i < last)(all_rows)
        pl.when(i == last)(first_rows)


def _prompt_attn(qlat, qrope, ckvb, krb, heads, tq, tks, scale, t_real):
    g_, r_, kv_rank = ckvb.shape
    rope = krb.shape[-1]
    assert all(tk % tq == 0 for tk in tks) and r_ % tq == 0 and tq % LANES == 0 and r_ - tq < t_real <= r_
    n_last = _round_up(t_real - (r_ - tq), 2 * SUBLANES)
    f = functools.partial(_prompt_attn_kernel, tq=tq, tks=tks, heads=heads, kv=kv_rank, rope=rope,
                          coef=scale * LOG2E, n_last=n_last)
    return pl.pallas_call(
        f, grid=(g_, r_ // tq),
        in_specs=[_row_spec(tq, heads * kv_rank), _row_spec(tq, heads * rope),
                  _stream_spec(r_, kv_rank), _stream_spec(r_, rope)],
        out_specs=_row_spec(tq, heads * kv_rank),
        out_shape=jax.ShapeDtypeStruct((g_, r_, heads * kv_rank), BF16),
        scratch_shapes=[pltpu.VMEM((heads, tq, LANES), F32), pltpu.VMEM((heads, tq, LANES), F32),
                        pltpu.VMEM((heads, tq, kv_rank), F32)],
        compiler_params=_params("parallel", "parallel"), name="prompt_attn",
    )(qlat, qrope, ckvb, krb)


def _sample_attn_kernel(pt_ref, ql_ref, qr_ref, cn_ref, kn_ref, cache_c, cache_rt, o_ref,
                        cbuf, rbuf, sem_c, sem_r, *, layer, n_pages, page, heads, t_new, cpages, coef):
    b = pl.program_id(0)
    nb = pl.num_programs(0)

    def page_copies(seq, slot, p):
        pg = pt_ref[seq * n_pages + p]
        return (pltpu.make_async_copy(cache_c.at[layer, pg], cbuf.at[slot, p], sem_c.at[slot]),
                pltpu.make_async_copy(cache_rt.at[layer, pg], rbuf.at[slot, p], sem_r.at[slot]))

    def start_fetch(seq, slot):
        def body(p, carry):
            for cp in page_copies(seq, slot, p):
                cp.start()
            return carry
        lax.fori_loop(0, n_pages, body, 0, unroll=DMA_UNROLL)

    def wait_fetch(seq, slot):
        def body(p, carry):
            for cp in page_copies(seq, slot, p):
                cp.wait()
            return carry
        lax.fori_loop(0, n_pages, body, 0, unroll=DMA_UNROLL)

    slot = lax.rem(b, 2)

    @pl.when(b == 0)
    def _():
        start_fetch(b, slot)

    @pl.when(b + 1 < nb)
    def _():
        start_fetch(b + 1, 1 - slot)

    wait_fetch(b, slot)

    ql = ql_ref[...]
    qr = qr_ref[...]
    kv = ql.shape[1]
    ck = cpages * page
    n_chunks = n_pages // cpages

    def scores(j):
        c = cbuf[slot, j * cpages:(j + 1) * cpages].reshape(ck, kv).astype(BF16)
        s_rope = jnp.concatenate([_dot(qr, rbuf[slot, j * cpages + p].astype(BF16)) for p in range(cpages)], axis=1)
        return _dot_nt(ql, c) + s_rope, c

    cn = cn_ref[...].astype(BF16)
    kn = kn_ref[...].astype(BF16)
    s_new = _dot_nt(ql, cn) + _dot_nt(qr, kn)
    q_tok = lax.broadcasted_iota(jnp.int32, s_new.shape, 0) // heads
    k_tok = lax.broadcasted_iota(jnp.int32, s_new.shape, 1)
    s_new = jnp.where((k_tok <= q_tok) & (k_tok < t_new), s_new, -jnp.inf)
    m = jnp.max(s_new, axis=1, keepdims=True)
    p_new = jnp.exp2((s_new - m) * coef)
    l = jnp.sum(p_new, axis=1, keepdims=True)
    acc = _dot(p_new.astype(BF16), cn)

    pend = [scores(j) for j in range(min(CHUNKS_AHEAD, n_chunks))]
    for j in range(n_chunks):
        s, c = pend.pop(0)
        if j + CHUNKS_AHEAD < n_chunks:
            pend.append(scores(j + CHUNKS_AHEAD))
        m_new = jnp.maximum(m, jnp.max(s, axis=1, keepdims=True))
        alpha = jnp.exp2((m - m_new) * coef)
        p = jnp.exp2((s - m_new) * coef)
        l = alpha * l + jnp.sum(p, axis=1, keepdims=True)
        acc = alpha * acc + _dot(p.astype(BF16), c)
        m = m_new
    o_ref[...] = (acc / l).astype(BF16)


def _sample_attn(page_table, qlat, qrope, c_new, k_new, cache_c, cache_rt, layer, heads, scale):
    bs, rows, kv_rank = qlat.shape
    rope = qrope.shape[-1]
    t_pad = c_new.shape[1]
    n_pages = page_table.shape[1]
    page = cache_c.shape[2]
    cpages = math.gcd(PAGES_PER_CHUNK, n_pages)
    f = functools.partial(_sample_attn_kernel, layer=layer, n_pages=n_pages, page=page, heads=heads,
                          t_new=rows // heads, cpages=cpages, coef=scale * LOG2E)
    seq = lambda r, c: pl.BlockSpec((None, r, c), lambda b, pt: (b, 0, 0))
    grid_spec = pltpu.PrefetchScalarGridSpec(
        num_scalar_prefetch=1, grid=(bs,),
        in_specs=[seq(rows, kv_rank), seq(rows, rope), seq(t_pad, kv_rank), seq(t_pad, rope),
                  pl.BlockSpec(memory_space=pl.ANY), pl.BlockSpec(memory_space=pl.ANY)],
        out_specs=seq(rows, kv_rank),
        scratch_shapes=[pltpu.VMEM((2, n_pages, page, kv_rank), F32), pltpu.VMEM((2, n_pages, rope, page), F32),
                        pltpu.SemaphoreType.DMA((2,)), pltpu.SemaphoreType.DMA((2,))],
    )
    return pl.pallas_call(
        f, grid_spec=grid_spec, out_shape=jax.ShapeDtypeStruct((bs, rows, kv_rank), BF16),
        compiler_params=_params("arbitrary"), name="sample_attn",
    )(page_table.reshape(-1), qlat, qrope, c_new, k_new, cache_c, cache_rt)


def _mix_out_kernel(o_ref, a_ref, ga_ref, gb_ref, x_ref, wuv_ref, wbr_ref, wout_ref,
                    nf_ref, st_ref, wup_ref, cw_ref, cb_ref, wdn_ref, nfin_ref,
                    x2_ref, ulast_ref, y_ref, ov, ubuf,
                    *, heads, kv_rank, v_dim, tile, pad, step, kconv, ff, fc, u_tile, u_off):
    t = pl.program_id(1)

    @pl.when(t == 0)
    def _():
        ubuf[0:pad, :] = st_ref[...]

    for hd in range(heads):
        ov[:, hd * v_dim:(hd + 1) * v_dim] = _dot(o_ref[:, hd * kv_rank:(hd + 1) * kv_rank], wuv_ref[hd]).astype(BF16)
    out_b = _dot(ov[...], wbr_ref[...])
    merged = _sigmoid(ga_ref[...]) * a_ref[...] + _sigmoid(gb_ref[...]) * out_b
    x1 = x_ref[...] + _dot(merged.astype(BF16), wout_ref[...])

    hf = _rms(x1, nf_ref[...]).astype(BF16)
    acc = x1

    def up(c):
        return _dot(hf, wup_ref[:, c * fc:(c + 1) * fc]), _dot(hf, wup_ref[:, ff + c * fc:ff + (c + 1) * fc])

    n_chunks = ff // fc
    pend = [up(c) for c in range(min(FFN_AHEAD, n_chunks))]
    for c in range(n_chunks):
        cs = slice(c * fc, (c + 1) * fc)
        u, v = pend.pop(0)
        if c + FFN_AHEAD < n_chunks:
            pend.append(up(c + FFN_AHEAD))
        ubuf[pad:pad + tile, cs] = u
        uc = cb_ref[:, cs] + cw_ref[kconv - 1:kconv, cs] * u
        for k in range(kconv - 1):
            off = pad - (kconv - 1 - k) * step
            uc = uc + cw_ref[k:k + 1, cs] * ubuf[off:off + tile, cs]
        acc = acc + _dot((_gelu(uc) * v).astype(BF16), wdn_ref[cs, :])
    x2_ref[...] = acc
    if y_ref is not None:
        y_ref[...] = _rms(acc, nfin_ref[...])

    @pl.when(t == u_tile)
    def _():
        ulast_ref[...] = ubuf[u_off:u_off + (kconv - 1) * step, :]

    ubuf[0:pad, :] = ubuf[tile:tile + pad, :]


def _mix_out(o_lat, out_a, g_a, g_b, x, conv_state, norm_final, wts, layer, tile, step, t_real, final):
    g_, r_, d = x.shape
    heads, kv_rank, v_dim = wts["w_uv"].shape[1:]
    kconv, ff = wts["ffn_conv_w"].shape[1:]
    pad = conv_state.shape[1]
    keep = (kconv - 1) * step
    fc = math.gcd(FFN_CHUNK, ff)
    assert pad % SUBLANES == 0 and pad >= keep and tile >= pad
    u_tile, u_off = _tail_rows(t_real, kconv - 1, step, tile, pad)
    body = functools.partial(_mix_out_kernel, heads=heads, kv_rank=kv_rank, v_dim=v_dim, tile=tile, pad=pad,
                             step=step, kconv=kconv, ff=ff, fc=fc, u_tile=u_tile, u_off=u_off)
    n_in = 15
    if final:
        f = body
    else:
        def f(*refs):
            return body(*refs[:n_in + 2], None, *refs[n_in + 2:])
    sds = jax.ShapeDtypeStruct
    out_specs = [_row_spec(tile, d), _stream_spec(keep, ff)]
    out_shape = [sds((g_, r_, d), F32), sds((g_, keep, ff), F32)]
    if final:
        out_specs.append(_row_spec(tile, d))
        out_shape.append(sds((g_, r_, d), F32))
    return pl.pallas_call(
        f, grid=(g_, r_ // tile),
        in_specs=[_row_spec(tile, heads * kv_rank), _row_spec(tile, d), _row_spec(tile, d), _row_spec(tile, d),
                  _row_spec(tile, d), _layer_spec(layer, heads, kv_rank, v_dim),
                  _layer_spec(layer, heads * v_dim, d), _layer_spec(layer, d, d),
                  _layer_spec(layer, 1, d), _stream_spec(pad, ff),
                  _layer_spec(layer, d, 2 * ff), _layer_spec(layer, kconv, ff), _layer_spec(layer, 1, ff),
                  _layer_spec(layer, ff, d), pl.BlockSpec((1, d), lambda g, t: (0, 0))],
        out_specs=tuple(out_specs), out_shape=tuple(out_shape),
        scratch_shapes=[pltpu.VMEM((tile, heads * v_dim), BF16), pltpu.VMEM((pad + tile, ff), F32)],
        compiler_params=_params("parallel", "arbitrary"), name="mix_out",
    )(o_lat, out_a, g_a, g_b, x, wts["w_uv"], wts["w_br_attn"], wts["w_out"], wts["norm_ffn"], conv_state,
      wts["ffn_w_up"], wts["ffn_conv_w"], wts["ffn_conv_b"], wts["ffn_w_down"], norm_final)


def _rope_tables(pos, half):
    inv = ROPE_THETA ** (-jnp.arange(half, dtype=F32) / half)
    ang = pos.astype(F32)[:, None] * inv[None, :]
    c, s = jnp.cos(ang), jnp.sin(ang)
    return jnp.concatenate([c, c, c, c], axis=1), jnp.concatenate([-s, s, -s, s], axis=1)


def _swap_halves(w):
    half = w.shape[-1] // 2
    return jnp.concatenate([w[..., half:], w[..., :half]], axis=-1)


def _pack_weights(p, d, q_rank, kv_rank, heads, nope, rope):
    depth = p["w_in"].shape[0]
    w_in = p["w_in"]
    o_cq = 2 * d
    o_kr = o_cq + q_rank + kv_rank
    o_ga = o_kr + rope
    k_r = w_in[:, :, o_kr:o_ga]
    w_in_p = jnp.concatenate([w_in[:, :, :2 * d], w_in[:, :, o_ga:o_ga + 2 * d], w_in[:, :, o_cq:o_kr],
                              k_r, _swap_halves(k_r)], axis=-1).astype(BF16)
    w_uq = p["w_uq"].reshape(depth, q_rank, heads, nope + rope)
    q_r = w_uq[..., nope:]
    w_uq_p = jnp.concatenate([w_uq[..., :nope].reshape(depth, q_rank, heads * nope),
                              q_r.reshape(depth, q_rank, heads * rope),
                              _swap_halves(q_r).reshape(depth, q_rank, heads * rope)], axis=-1).astype(BF16)
    row = lambda a: a[:, None, :]
    return {
        "norm_mix": row(p["norm_mix"]), "w_in": w_in_p, "q_norm": row(p["q_norm"]), "kv_norm": row(p["kv_norm"]),
        "w_uq": w_uq_p, "w_uk": p["w_uk"].transpose(0, 2, 3, 1).astype(BF16),
        "w_uv": p["w_uv"].transpose(0, 2, 1, 3).astype(BF16),
        "rnn_conv_w": p["rnn_conv_w"], "rnn_conv_b": row(p["rnn_conv_b"]),
        "rnn_wg": jnp.concatenate([p["rnn_wa"], p["rnn_wx"]], axis=-1).astype(BF16),
        "rnn_ba": row(p["rnn_ba"]), "rnn_bx": row(p["rnn_bx"]), "rnn_lambda": row(p["rnn_lambda"]),
        "w_br_rnn": p["w_br_rnn"].astype(BF16), "w_br_attn": p["w_br_attn"].astype(BF16),
        "w_out": p["w_out"].astype(BF16), "norm_ffn": row(p["norm_ffn"]),
        "ffn_w_up": p["ffn_w_up"].astype(BF16), "ffn_conv_w": p["ffn_conv_w"],
        "ffn_conv_b": row(p["ffn_conv_b"]), "ffn_w_down": p["ffn_w_down"].astype(BF16),
    }


def _pad_rows(a, rows):
    return jnp.pad(a, ((0, 0), (rows - a.shape[1], 0), (0, 0)))


def kernel(x_prompt, x_sample, cache_kv_latent, cache_k_rope, state_rnn_h, state_rnn_conv, state_ffn_conv,
           page_table, meta_tokens, norm_mix, w_in, rnn_conv_w, rnn_conv_b, rnn_wa, rnn_ba, rnn_wx, rnn_bx,
           rnn_lambda, q_norm, w_uq, kv_norm, w_uk, w_uv, w_br_rnn, w_br_attn, w_out, norm_ffn, ffn_w_up,
           ffn_conv_w, ffn_conv_b, ffn_w_down, norm_final):
    bp, seq, d = x_prompt.shape
    bs, ts, _ = x_sample.shape
    depth = w_in.shape[0]
    n_meta = meta_tokens.shape[0]
    kv_rank, heads, nope = w_uk.shape[1:]
    q_rank = w_uq.shape[1]
    rope = w_uq.shape[2] // heads - nope
    ff = ffn_w_down.shape[1]
    rnn_k = rnn_conv_w.shape[1]
    ffn_k = ffn_conv_w.shape[1]
    past_len = page_table.shape[1] * cache_kv_latent.shape[2]
    scale = 1.0 / math.sqrt(nope + rope)
    assert 2 * rope == LANES and kv_rank % LANES == 0 and nope % LANES == 0 and d % MXU_WIDTH == 0
    dims = (q_rank, kv_rank, heads, nope, rope)

    wts = _pack_weights(
        dict(norm_mix=norm_mix, w_in=w_in, rnn_conv_w=rnn_conv_w, rnn_conv_b=rnn_conv_b, rnn_wa=rnn_wa,
             rnn_ba=rnn_ba, rnn_wx=rnn_wx, rnn_bx=rnn_bx, rnn_lambda=rnn_lambda, q_norm=q_norm, w_uq=w_uq,
             kv_norm=kv_norm, w_uk=w_uk, w_uv=w_uv, w_br_rnn=w_br_rnn, w_br_attn=w_br_attn, w_out=w_out,
             norm_ffn=norm_ffn, ffn_w_up=ffn_w_up, ffn_conv_w=ffn_conv_w, ffn_conv_b=ffn_conv_b,
             ffn_w_down=ffn_w_down),
        d, q_rank, kv_rank, heads, nope, rope)
    nfin = norm_final[None]

    lp = seq + n_meta
    lpad = _round_up(lp, PROMPT_TILE)
    xp = jnp.concatenate([jnp.broadcast_to(meta_tokens[None].astype(x_prompt.dtype), (bp, n_meta, d)), x_prompt,
                          jnp.zeros((bp, lpad - lp, d), x_prompt.dtype)], axis=1)
    cos_p, sin_p = _rope_tables(jnp.arange(lpad, dtype=jnp.int32), rope // 2)
    zeros_p = lambda c: jnp.zeros((bp, SUBLANES, c), F32)
    rs = ts * bs
    xs = x_sample.transpose(1, 0, 2).reshape(1, rs, d)
    cos_s, sin_s = _rope_tables(past_len + jnp.repeat(jnp.arange(ts, dtype=jnp.int32), bs), rope // 2)
    t_pad = _round_up(ts, 2 * SUBLANES)
    cache_rt = cache_k_rope.transpose(0, 1, 3, 2)

    def time_major(state):
        return state.transpose(1, 0, 2).reshape(1, state.shape[1] * bs, state.shape[2])

    def batch_major(a, k):
        return a.reshape(k, bs, a.shape[-1]).transpose(1, 0, 2)

    outs = {k: [] for k in ("p_lat", "p_rope", "p_h", "p_rc", "p_fc", "s_lat", "s_rope", "s_h", "s_rc", "s_fc")}
    y_p = y_s = None
    for l in range(depth):
        final = l == depth - 1
        out_a, g_a, g_b, ckv, ckvb, kr, krb, qlat, qrope, h_last, x_last = _mix_in(
            xp, cos_p, sin_p, zeros_p(d), jnp.zeros((bp, 1, d), F32), wts, l, PROMPT_TILE, 1, lp, dims)
        o_lat = _prompt_attn(qlat, qrope, ckvb, krb, heads, PROMPT_TILE, ATTN_KEY_TILES, scale, lp)
        res = _mix_out(o_lat, out_a, g_a, g_b, xp, zeros_p(ff), nfin, wts, l, PROMPT_TILE, 1, lp, final)
        xp, u_last = res[0], res[1]
        if final:
            y_p = res[2]
        outs["p_lat"].append(ckv[:, :lp])
        outs["p_rope"].append(kr[:, :lp])
        outs["p_h"].append(h_last[:, 0])
        outs["p_rc"].append(x_last)
        outs["p_fc"].append(u_last)

        rc0 = time_major(state_rnn_conv[l])
        out_a, g_a, g_b, ckv, ckvb, kr, krb, qlat, qrope, h_last, x_last = _mix_in(
            xs, cos_s, sin_s, _pad_rows(rc0, _round_up(rc0.shape[1], SUBLANES)), state_rnn_h[l][None],
            wts, l, rs, bs, ts, dims)
        to_seq = lambda a: jnp.pad(batch_major(a, ts), ((0, 0), (0, t_pad - ts), (0, 0)))
        o_seq = _sample_attn(page_table, batch_major(qlat, ts).reshape(bs, ts * heads, kv_rank),
                             batch_major(qrope, ts).reshape(bs, ts * heads, rope),
                             to_seq(ckv), to_seq(kr), cache_kv_latent, cache_rt, l, heads, scale)
        o_lat = o_seq.reshape(bs, ts, heads * kv_rank).transpose(1, 0, 2).reshape(1, rs, heads * kv_rank)
        fc0 = time_major(state_ffn_conv[l])
        res = _mix_out(o_lat, out_a, g_a, g_b, xs, _pad_rows(fc0, _round_up(fc0.shape[1], SUBLANES)), nfin,
                       wts, l, rs, bs, ts, final)
        xs, u_last = res[0], res[1]
        if final:
            y_s = res[2]
        outs["s_lat"].append(batch_major(ckv, ts))
        outs["s_rope"].append(batch_major(kr, ts))
        outs["s_h"].append(h_last[0])
        outs["s_rc"].append(batch_major(x_last, rnn_k - 1))
        outs["s_fc"].append(batch_major(u_last, ffn_k - 1))

    y_prompt = y_p[:, n_meta:lp]
    y_sample = batch_major(y_s, ts)
    st = {k: jnp.stack(v) for k, v in outs.items()}
    return (y_prompt, y_sample, st["p_lat"], st["p_rope"], st["p_h"], st["p_rc"], st["p_fc"],
            st["s_lat"], st["s_rope"], st["s_h"], st["s_rc"], st["s_fc"])
```

```python
import functools
import math

import jax
import jax.numpy as jnp
from jax import lax
from jax.experimental import pallas as pl
from jax.experimental.pallas import tpu as pltpu

F32 = jnp.float32
BF16 = jnp.bfloat16

EPS = 1e-6
RGLRU_C = 8.0
ROPE_THETA = 10000.0
LANES = 128
SUBLANES = 8
MXU_WIDTH = 256
VMEM_LIMIT = 56 * 1024 * 1024
LOG2E = 1.4426950408889634
PROMPT_TILE = 256
ATTN_KEY_TILES = (1024, 512)
HEADS_AHEAD = 3
PAGES_PER_CHUNK = 8
CHUNKS_AHEAD = 2
DMA_UNROLL = 8
FFN_CHUNK = 512
RNN_PIECE = 32
FFN_AHEAD = 3


def _round_up(x, m):
    return (x + m - 1) // m * m


def _rms(x, g):
    return x * lax.rsqrt(jnp.mean(x * x, axis=-1, keepdims=True) + EPS) * g


def _gelu(x):
    return 0.5 * x * (1.0 + jnp.tanh(math.sqrt(2.0 / math.pi) * (x + 0.044715 * (x * x * x))))


def _sigmoid(x):
    return 1.0 / (1.0 + jnp.exp(-x))


def _dot(a, b):
    return jnp.dot(a, b, preferred_element_type=F32)


def _dot_nt(a, b):
    return lax.dot_general(a, b, (((1,), (1,)), ((), ())), preferred_element_type=F32)


def _params(*sem):
    return pltpu.CompilerParams(dimension_semantics=sem, vmem_limit_bytes=VMEM_LIMIT)


def _row_spec(tile, cols):
    return pl.BlockSpec((None, tile, cols), lambda g, t: (g, t, 0))


def _stream_spec(rows, cols):
    return pl.BlockSpec((None, rows, cols), lambda g, t: (g, 0, 0))


def _layer_spec(layer, *shape):
    zeros = (0,) * len(shape)
    return pl.BlockSpec((None,) + tuple(shape), lambda g, t: (layer,) + zeros)


def _tail_rows(t_real, keep, step, tile, pad):
    first_row = (t_real - keep) * step
    last_tile = (first_row + keep * step - 1) // tile
    off = pad + first_row - last_tile * tile
    assert off >= 0
    return last_tile, off


def _scan_rows(a, u, step):
    n = a.shape[0]
    row = lax.broadcasted_iota(jnp.int32, a.shape, 0)
    d = step
    while d < n:
        keep = row >= d
        a_prev = jnp.where(keep, pltpu.roll(a, d, 0), 1.0)
        u_prev = jnp.where(keep, pltpu.roll(u, d, 0), 0.0)
        u = a * u_prev + u
        a = a * a_prev
        d *= 2
    return a, u


def _recurrence(a, u, h_in, step):
    n, width = a.shape
    if step > 1:
        a_p, u_p = _scan_rows(a, u, step)
        return a_p * jnp.concatenate([h_in] * (n // step), axis=0) + u_p
    groups = n // SUBLANES
    a3 = a.reshape(groups, SUBLANES, width)
    u3 = u.reshape(groups, SUBLANES, width)
    sub = lax.broadcasted_iota(jnp.int32, a3.shape, 1)
    d = 1
    while d < SUBLANES:
        keep = sub >= d
        a_prev = jnp.where(keep, pltpu.roll(a3, d, 1), 1.0)
        u_prev = jnp.where(keep, pltpu.roll(u3, d, 1), 0.0)
        u3 = a3 * u_prev + u3
        a3 = a3 * a_prev
        d *= 2
    carry = h_in
    out = []
    for g in range(groups):
        h = a3[g] * carry + u3[g]
        out.append(h)
        carry = h[SUBLANES - 1:SUBLANES, :]
    return jnp.concatenate(out, axis=0)


def _mix_in_kernel(x_ref, nm_ref, w_ref, qn_ref, kvn_ref, wuq_ref, wuk_ref, cos_ref, sin_ref,
                   st_ref, h0_ref, cw_ref, cb_ref, wg_ref, ba_ref, bx_ref, lam_ref, wbr_ref,
                   outa_ref, ga_ref, gb_ref, ckv_ref, ckvb_ref, kr_ref, krb_ref, qlat_ref, qrope_ref,
                   hlast_ref, xlast_ref, xbuf, gbuf, xcbuf, gbuf2, hcar, gated, cqn, qnope,
                   *, tile, pad, step, kconv, blk, d, q_rank, kv_rank, heads, nope, rope,
                   h_tile, h_off, x_tile, x_off):
    t = pl.program_id(1)

    @pl.when(t == 0)
    def _():
        xbuf[0:pad, :] = st_ref[...]
        hcar[...] = h0_ref[...]

    h = _rms(x_ref[...], nm_ref[...]).astype(BF16)

    def proj(c0, c1):
        return _dot(h, w_ref[:, c0:c1])

    xbuf[pad:pad + tile, :] = proj(0, d)
    gbuf[...] = proj(d, 2 * d)
    cos = cos_ref[...]
    sin = sin_ref[...]

    for n in range(d // blk):
        cs = slice(n * blk, (n + 1) * blk)
        xc = cb_ref[:, cs] + cw_ref[kconv - 1:kconv, cs] * xbuf[pad:pad + tile, cs]
        for k in range(kconv - 1):
            off = pad - (kconv - 1 - k) * step
            xc = xc + cw_ref[k:k + 1, cs] * xbuf[off:off + tile, cs]
        xcbuf[:, cs] = xc
        gbuf2[:, 2 * n * blk:2 * (n + 1) * blk] = _dot(xc.astype(BF16), wg_ref[n])

    piece = RNN_PIECE if step == 1 else tile

    def rnn_piece(n, rc):
        def run():
            cs = slice(n * blk, (n + 1) * blk)
            rs = slice(rc * piece, (rc + 1) * piece)
            xc = xcbuf[rs, cs]
            r = _sigmoid(gbuf2[rs, 2 * n * blk:(2 * n + 1) * blk] + ba_ref[:, cs])
            i = _sigmoid(gbuf2[rs, (2 * n + 1) * blk:2 * (n + 1) * blk] + bx_ref[:, cs])
            neg_lam = -lam_ref[:, cs]
            softplus = jnp.maximum(neg_lam, 0.0) + jnp.log(1.0 + jnp.exp(-jnp.abs(neg_lam)))
            log_a = -RGLRU_C * r * softplus
            a = jnp.exp(log_a)
            y = 1.0 - jnp.exp(2.0 * log_a)
            root = jnp.where(y > 0.0, y * lax.rsqrt(y), 0.0)
            u = root * (i * xc)
            hs = _recurrence(a, u, hcar[:, cs], step)
            hcar[:, cs] = hs[piece - step:piece, :]
            if rc == h_off // piece:
                @pl.when(t == h_tile)
                def _():
                    hlast_ref[:, cs] = hs[h_off % piece:h_off % piece + step, :]

            gated[rs, cs] = (hs * _gelu(gbuf[rs, cs])).astype(BF16)
        return run

    def gate_cols(ref, c0, c):
        def run():
            ref[:, c:c + MXU_WIDTH] = proj(c0 + c, c0 + c + MXU_WIDTH)
        return run

    def latent_and_key():
        o = 4 * d
        cqn[...] = _rms(proj(o, o + q_rank), qn_ref[...]).astype(BF16)
        o += q_rank
        c_kv = _rms(proj(o, o + kv_rank), kvn_ref[...])
        o += kv_rank
        ckv_ref[...] = c_kv
        ckvb_ref[...] = c_kv.astype(BF16)
        kk = proj(o, o + 2 * rope)
        k_rot = (kk * cos + pltpu.roll(kk, rope, 1) * sin)[:, :rope]
        kr_ref[...] = k_rot
        krb_ref[...] = k_rot.astype(BF16)

    hn = heads * nope
    hr = heads * rope

    def q_rope_cols(c):
        def run():
            q = cqn[...]
            qr = _dot(q, wuq_ref[:, hn + c:hn + c + MXU_WIDTH])
            qp = _dot(q, wuq_ref[:, hn + hr + c:hn + hr + c + MXU_WIDTH])
            for j in range(0, MXU_WIDTH, LANES):
                qrope_ref[:, c + j:c + j + LANES] = (qr[:, j:j + LANES] * cos + qp[:, j:j + LANES] * sin).astype(BF16)
        return run

    def q_nope_cols(c):
        def run():
            qnope[:, c:c + MXU_WIDTH] = _dot(cqn[...], wuq_ref[:, c:c + MXU_WIDTH]).astype(BF16)
        return run

    def q_lat_head(hd):
        def run():
            qlat_ref[:, hd * kv_rank:(hd + 1) * kv_rank] = _dot(qnope[:, hd * nope:(hd + 1) * nope],
                                                                wuk_ref[hd]).astype(BF16)
        return run

    others = [latent_and_key]
    others += [gate_cols(ga_ref, 2 * d, c) for c in range(0, d, MXU_WIDTH)]
    others += [gate_cols(gb_ref, 3 * d, c) for c in range(0, d, MXU_WIDTH)]
    others += [q_nope_cols(c) for c in range(0, hn, MXU_WIDTH)]
    others += [q_rope_cols(c) for c in range(0, hr, MXU_WIDTH)] + [q_lat_head(hd) for hd in range(heads)]
    n_blocks = d // blk
    n_pieces = tile // piece
    per_dot = MXU_WIDTH // blk
    vec = [rnn_piece(n, rc) for n in range(n_blocks) for rc in range(n_pieces)]
    out_a = None
    for k, item in enumerate(vec):
        item()
        for other in others[k * len(others) // len(vec):(k + 1) * len(others) // len(vec)]:
            other()
        if (k + 1) % (n_pieces * per_dot) == 0:
            n_done = (k + 1) // n_pieces
            ks = slice((n_done - per_dot) * blk, n_done * blk)
            part = _dot(gated[:, ks], wbr_ref[ks, :])
            out_a = part if out_a is None else out_a + part
    outa_ref[...] = out_a

    @pl.when(t == x_tile)
    def _():
        xlast_ref[...] = xbuf[x_off:x_off + (kconv - 1) * step, :]

    xbuf[0:pad, :] = xbuf[tile:tile + pad, :]


def _mix_in(x, cos4, sin4, conv_state, h0, wts, layer, tile, step, t_real, dims):
    g_, r_, d = x.shape
    q_rank, kv_rank, heads, nope, rope = dims
    width = wts["w_in"].shape[-1]
    kconv = wts["rnn_conv_w"].shape[1]
    nb, blk = wts["rnn_wg"].shape[1], wts["rnn_wg"].shape[2]
    pad = conv_state.shape[1]
    keep = (kconv - 1) * step
    assert pad % SUBLANES == 0 and pad >= keep and tile >= pad and blk == LANES and tile % SUBLANES == 0
    h_tile, h_off = _tail_rows(t_real, 1, step, tile, 0)
    x_tile, x_off = _tail_rows(t_real, kconv - 1, step, tile, pad)
    f = functools.partial(_mix_in_kernel, tile=tile, pad=pad, step=step, kconv=kconv, blk=blk, d=d,
                          q_rank=q_rank, kv_rank=kv_rank, heads=heads, nope=nope, rope=rope,
                          h_tile=h_tile, h_off=h_off, x_tile=x_tile, x_off=x_off)
    tab = pl.BlockSpec((tile, LANES), lambda g, t: (t, 0))
    sds = jax.ShapeDtypeStruct
    out_shape = (
        sds((g_, r_, d), F32), sds((g_, r_, d), F32), sds((g_, r_, d), F32),
        sds((g_, t_real * step, kv_rank), F32), sds((g_, r_, kv_rank), BF16),
        sds((g_, t_real * step, rope), F32), sds((g_, r_, rope), BF16),
        sds((g_, r_, heads * kv_rank), BF16), sds((g_, r_, heads * rope), BF16),
        sds((g_, step, d), F32), sds((g_, keep, d), F32),
    )
    out_specs = (
        _row_spec(tile, d), _row_spec(tile, d), _row_spec(tile, d),
        _row_spec(tile, kv_rank), _row_spec(tile, kv_rank), _row_spec(tile, rope), _row_spec(tile, rope),
        _row_spec(tile, heads * kv_rank), _row_spec(tile, heads * rope),
        _stream_spec(step, d), _stream_spec(keep, d),
    )
    in_specs = [
        _row_spec(tile, d), _layer_spec(layer, 1, d), _layer_spec(layer, d, width),
        _layer_spec(layer, 1, q_rank), _layer_spec(layer, 1, kv_rank),
        _layer_spec(layer, q_rank, wts["w_uq"].shape[-1]), _layer_spec(layer, heads, nope, kv_rank),
        tab, tab, _stream_spec(pad, d), _stream_spec(step, d),
        _layer_spec(layer, kconv, d), _layer_spec(layer, 1, d), _layer_spec(layer, nb, blk, 2 * blk),
        _layer_spec(layer, 1, d), _layer_spec(layer, 1, d), _layer_spec(layer, 1, d), _layer_spec(layer, d, d),
    ]
    return pl.pallas_call(
        f, grid=(g_, r_ // tile), in_specs=in_specs, out_specs=out_specs, out_shape=out_shape,
        scratch_shapes=[pltpu.VMEM((pad + tile, d), F32), pltpu.VMEM((tile, d), F32), pltpu.VMEM((tile, d), F32),
                        pltpu.VMEM((tile, 2 * d), F32), pltpu.VMEM((step, d), F32),
                        pltpu.VMEM((tile, d), BF16), pltpu.VMEM((tile, q_rank), BF16),
                        pltpu.VMEM((tile, heads * nope), BF16)],
        compiler_params=_params("parallel", "arbitrary"), name="mix_in",
    )(x, wts["norm_mix"], wts["w_in"], wts["q_norm"], wts["kv_norm"], wts["w_uq"], wts["w_uk"], cos4, sin4,
      conv_state, h0, wts["rnn_conv_w"], wts["rnn_conv_b"], wts["rnn_wg"], wts["rnn_ba"], wts["rnn_bx"],
      wts["rnn_lambda"], wts["w_br_rnn"])


def _prompt_attn_kernel(ql_ref, qr_ref, c_ref, kr_ref, o_ref, m_sc, l_sc, acc_sc,
                        *, tq, tks, heads, kv, rope, coef, n_last):
    i = pl.program_id(1)

    def attend(chains):
        for idx, rows, _, _, _ in chains:
            m_sc[idx, 0:rows] = jnp.full((rows, LANES), -jnp.inf, F32)
            l_sc[idx, 0:rows] = jnp.zeros((rows, LANES), F32)
            acc_sc[idx, 0:rows] = jnp.zeros((rows, kv), F32)

        def step(k0, width, masked):
            c = c_ref[pl.ds(k0, width), :]
            kr = kr_ref[pl.ds(k0, width), :]

            def scores(n):
                return _dot_nt(chains[n][2](), c) + _dot_nt(chains[n][3](), kr)

            pend = [scores(n) for n in range(min(HEADS_AHEAD, len(chains)))]
            for n, (idx, rows, _, _, token) in enumerate(chains):
                s = pend.pop(0)
                if n + HEADS_AHEAD < len(chains):
                    pend.append(scores(n + HEADS_AHEAD))
                if masked:
                    s = jnp.where(lax.broadcasted_iota(jnp.int32, (rows, width), 1) <= token(width), s, -jnp.inf)
                m_prev = m_sc[idx, 0:rows]
                m_new = jnp.maximum(m_prev, jnp.max(s, axis=1, keepdims=True))
                alpha = jnp.exp2((m_prev - m_new) * coef)
                p = jnp.exp2((s - jnp.concatenate([m_new] * (width // LANES), axis=1)) * coef)
                psum = p[:, :LANES]
                for t in range(1, width // LANES):
                    psum = psum + p[:, t * LANES:(t + 1) * LANES]
                l_sc[idx, 0:rows] = alpha * l_sc[idx, 0:rows] + psum
                acc_sc[idx, 0:rows] = (jnp.concatenate([alpha] * (kv // LANES), axis=1) * acc_sc[idx, 0:rows]
                                       + _dot(p.astype(BF16), c))
                m_sc[idx, 0:rows] = m_new

        done = 0
        for width in tuple(tks) + (tq,):
            ratio = width // tq
            n_tiles = (i - done) // ratio

            def body(j, carry, width=width, ratio=ratio, done=done):
                step(pl.multiple_of((done + j * ratio) * tq, tq), width, False)
                return carry

            lax.fori_loop(0, n_tiles, body, 0)
            done = done + n_tiles * ratio
        step(pl.multiple_of(i * tq, tq), tq, True)

    def normalised(idx, rows):
        return acc_sc[idx, 0:rows] / jnp.sum(l_sc[idx, 0:rows], axis=1, keepdims=True)

    def all_rows():
        attend([(h, tq, lambda h=h: ql_ref[:, h * kv:(h + 1) * kv], lambda h=h: qr_ref[:, h * rope:(h + 1) * rope],
                 lambda width: lax.broadcasted_iota(jnp.int32, (tq, width), 0)) for h in range(heads)])
        for h in range(heads):
            o_ref[:, h * kv:(h + 1) * kv] = normalised(h, tq).astype(BF16)

    def first_rows():
        rows = heads * n_last
        ql = jnp.concatenate([ql_ref[0:n_last, h * kv:(h + 1) * kv] for h in range(heads)], axis=0)
        qr = jnp.concatenate([qr_ref[0:n_last, h * rope:(h + 1) * rope] for h in range(heads)], axis=0)
        token = lambda width: jnp.concatenate([lax.broadcasted_iota(jnp.int32, (n_last, width), 0)] * heads, axis=0)
        attend([(0, rows, lambda: ql, lambda: qr, token)])
        o = normalised(0, rows).astype(BF16)
        o_ref[...] = jnp.zeros(o_ref.shape, BF16)
        for h in range(heads):
            o_ref[0:n_last, h * kv:(h + 1) * kv] = o[h * n_last:(h + 1) * n_last, :]

    if n_last == tq or heads * n_last > tq:
        all_rows()
    else:
        last = pl.num_programs(1) - 1
        pl.when(i < last)(all_rows)
        pl.when(i == last)(first_rows)


def _prompt_attn(qlat, qrope, ckvb, krb, heads, tq, tks, scale, t_real):
    g_, r_, kv_rank = ckvb.shape
    rope = krb.shape[-1]
    assert all(tk % tq == 0 for tk in tks) and r_ % tq == 0 and tq % LANES == 0 and r_ - tq < t_real <= r_
    n_last = _round_up(t_real - (r_ - tq), 2 * SUBLANES)
    f = functools.partial(_prompt_attn_kernel, tq=tq, tks=tks, heads=heads, kv=kv_rank, rope=rope,
                          coef=scale * LOG2E, n_last=n_last)
    return pl.pallas_call(
        f, grid=(g_, r_ // tq),
        in_specs=[_row_spec(tq, heads * kv_rank), _row_spec(tq, heads * rope),
                  _stream_spec(r_, kv_rank), _stream_spec(r_, rope)],
        out_specs=_row_spec(tq, heads * kv_rank),
        out_shape=jax.ShapeDtypeStruct((g_, r_, heads * kv_rank), BF16),
        scratch_shapes=[pltpu.VMEM((heads, tq, LANES), F32), pltpu.VMEM((heads, tq, LANES), F32),
                        pltpu.VMEM((heads, tq, kv_rank), F32)],
        compiler_params=_params("parallel", "parallel"), name="prompt_attn",
    )(qlat, qrope, ckvb, krb)


def _sample_attn_kernel(pt_ref, ql_ref, qr_ref, cn_ref, kn_ref, cache_c, cache_rt, o_ref,
                        cbuf, rbuf, sem_c, sem_r, *, layer, n_pages, page, heads, t_new, cpages, coef):
    b = pl.program_id(0)
    nb = pl.num_programs(0)

    def page_copies(seq, slot, p):
        pg = pt_ref[seq * n_pages + p]
        return (pltpu.make_async_copy(cache_c.at[layer, pg], cbuf.at[slot, p], sem_c.at[slot]),
                pltpu.make_async_copy(cache_rt.at[layer, pg], rbuf.at[slot, p], sem_r.at[slot]))

    def start_fetch(seq, slot):
        def body(p, carry):
            for cp in page_copies(seq, slot, p):
                cp.start()
            return carry
        lax.fori_loop(0, n_pages, body, 0, unroll=DMA_UNROLL)

    def wait_fetch(seq, slot):
        def body(p, carry):
            for cp in page_copies(seq, slot, p):
                cp.wait()
            return carry
        lax.fori_loop(0, n_pages, body, 0, unroll=DMA_UNROLL)

    slot = lax.rem(b, 2)

    @pl.when(b == 0)
    def _():
        start_fetch(b, slot)

    @pl.when(b + 1 < nb)
    def _():
        start_fetch(b + 1, 1 - slot)

    wait_fetch(b, slot)

    ql = ql_ref[...]
    qr = qr_ref[...]
    kv = ql.shape[1]
    ck = cpages * page
    n_chunks = n_pages // cpages

    def scores(j):
        c = cbuf[slot, j * cpages:(j + 1) * cpages].reshape(ck, kv).astype(BF16)
        s_rope = jnp.concatenate([_dot(qr, rbuf[slot, j * cpages + p].astype(BF16)) for p in range(cpages)], axis=1)
        return _dot_nt(ql, c) + s_rope, c

    cn = cn_ref[...].astype(BF16)
    kn = kn_ref[...].astype(BF16)
    s_new = _dot_nt(ql, cn) + _dot_nt(qr, kn)
    q_tok = lax.broadcasted_iota(jnp.int32, s_new.shape, 0) // heads
    k_tok = lax.broadcasted_iota(jnp.int32, s_new.shape, 1)
    s_new = jnp.where((k_tok <= q_tok) & (k_tok < t_new), s_new, -jnp.inf)
    m = jnp.max(s_new, axis=1, keepdims=True)
    p_new = jnp.exp2((s_new - m) * coef)
    l = jnp.sum(p_new, axis=1, keepdims=True)
    acc = _dot(p_new.astype(BF16), cn)

    pend = [scores(j) for j in range(min(CHUNKS_AHEAD, n_chunks))]
    for j in range(n_chunks):
        s, c = pend.pop(0)
        if j + CHUNKS_AHEAD < n_chunks:
            pend.append(scores(j + CHUNKS_AHEAD))
        m_new = jnp.maximum(m, jnp.max(s, axis=1, keepdims=True))
        alpha = jnp.exp2((m - m_new) * coef)
        p = jnp.exp2((s - m_new) * coef)
        l = alpha * l + jnp.sum(p, axis=1, keepdims=True)
        acc = alpha * acc + _dot(p.astype(BF16), c)
        m = m_new
    o_ref[...] = (acc / l).astype(BF16)


def _sample_attn(page_table, qlat, qrope, c_new, k_new, cache_c, cache_rt, layer, heads, scale):
    bs, rows, kv_rank = qlat.shape
    rope = qrope.shape[-1]
    t_pad = c_new.shape[1]
    n_pages = page_table.shape[1]
    page = cache_c.shape[2]
    cpages = math.gcd(PAGES_PER_CHUNK, n_pages)
    f = functools.partial(_sample_attn_kernel, layer=layer, n_pages=n_pages, page=page, heads=heads,
                          t_new=rows // heads, cpages=cpages, coef=scale * LOG2E)
    seq = lambda r, c: pl.BlockSpec((None, r, c), lambda b, pt: (b, 0, 0))
    grid_spec = pltpu.PrefetchScalarGridSpec(
        num_scalar_prefetch=1, grid=(bs,),
        in_specs=[seq(rows, kv_rank), seq(rows, rope), seq(t_pad, kv_rank), seq(t_pad, rope),
                  pl.BlockSpec(memory_space=pl.ANY), pl.BlockSpec(memory_space=pl.ANY)],
        out_specs=seq(rows, kv_rank),
        scratch_shapes=[pltpu.VMEM((2, n_pages, page, kv_rank), F32), pltpu.VMEM((2, n_pages, rope, page), F32),
                        pltpu.SemaphoreType.DMA((2,)), pltpu.SemaphoreType.DMA((2,))],
    )
    return pl.pallas_call(
        f, grid_spec=grid_spec, out_shape=jax.ShapeDtypeStruct((bs, rows, kv_rank), BF16),
        compiler_params=_params("arbitrary"), name="sample_attn",
    )(page_table.reshape(-1), qlat, qrope, c_new, k_new, cache_c, cache_rt)


def _mix_out_kernel(o_ref, a_ref, ga_ref, gb_ref, x_ref, wuv_ref, wbr_ref, wout_ref,
                    nf_ref, st_ref, wup_ref, cw_ref, cb_ref, wdn_ref, nfin_ref,
                    x2_ref, ulast_ref, y_ref, ov, ubuf,
                    *, heads, kv_rank, v_dim, tile, pad, step, kconv, ff, fc, u_tile, u_off):
    t = pl.program_id(1)

    @pl.when(t == 0)
    def _():
        ubuf[0:pad, :] = st_ref[...]

    for hd in range(heads):
        ov[:, hd * v_dim:(hd + 1) * v_dim] = _dot(o_ref[:, hd * kv_rank:(hd + 1) * kv_rank], wuv_ref[hd]).astype(BF16)
    out_b = _dot(ov[...], wbr_ref[...])
    merged = _sigmoid(ga_ref[...]) * a_ref[...] + _sigmoid(gb_ref[...]) * out_b
    x1 = x_ref[...] + _dot(merged.astype(BF16), wout_ref[...])

    hf = _rms(x1, nf_ref[...]).astype(BF16)
    acc = x1

    def up(c):
        return _dot(hf, wup_ref[:, c * fc:(c + 1) * fc]), _dot(hf, wup_ref[:, ff + c * fc:ff + (c + 1) * fc])

    n_chunks = ff // fc
    pend = [up(c) for c in range(min(FFN_AHEAD, n_chunks))]
    for c in range(n_chunks):
        cs = slice(c * fc, (c + 1) * fc)
        u, v = pend.pop(0)
        if c + FFN_AHEAD < n_chunks:
            pend.append(up(c + FFN_AHEAD))
        ubuf[pad:pad + tile, cs] = u
        uc = cb_ref[:, cs] + cw_ref[kconv - 1:kconv, cs] * u
        for k in range(kconv - 1):
            off = pad - (kconv - 1 - k) * step
            uc = uc + cw_ref[k:k + 1, cs] * ubuf[off:off + tile, cs]
        acc = acc + _dot((_gelu(uc) * v).astype(BF16), wdn_ref[cs, :])
    x2_ref[...] = acc
    if y_ref is not None:
        y_ref[...] = _rms(acc, nfin_ref[...])

    @pl.when(t == u_tile)
    def _():
        ulast_ref[...] = ubuf[u_off:u_off + (kconv - 1) * step, :]

    ubuf[0:pad, :] = ubuf[tile:tile + pad, :]


def _mix_out(o_lat, out_a, g_a, g_b, x, conv_state, norm_final, wts, layer, tile, step, t_real, final):
    g_, r_, d = x.shape
    heads, kv_rank, v_dim = wts["w_uv"].shape[1:]
    kconv, ff = wts["ffn_conv_w"].shape[1:]
    pad = conv_state.shape[1]
    keep = (kconv - 1) * step
    fc = math.gcd(FFN_CHUNK, ff)
    assert pad % SUBLANES == 0 and pad >= keep and tile >= pad
    u_tile, u_off = _tail_rows(t_real, kconv - 1, step, tile, pad)
    body = functools.partial(_mix_out_kernel, heads=heads, kv_rank=kv_rank, v_dim=v_dim, tile=tile, pad=pad,
                             step=step, kconv=kconv, ff=ff, fc=fc, u_tile=u_tile, u_off=u_off)
    n_in = 15
    if final:
        f = body
    else:
        def f(*refs):
            return body(*refs[:n_in + 2], None, *refs[n_in + 2:])
    sds = jax.ShapeDtypeStruct
    out_specs = [_row_spec(tile, d), _stream_spec(keep, ff)]
    out_shape = [sds((g_, r_, d), F32), sds((g_, keep, ff), F32)]
    if final:
        out_specs.append(_row_spec(tile, d))
        out_shape.append(sds((g_, r_, d), F32))
    return pl.pallas_call(
        f, grid=(g_, r_ // tile),
        in_specs=[_row_spec(tile, heads * kv_rank), _row_spec(tile, d), _row_spec(tile, d), _row_spec(tile, d),
                  _row_spec(tile, d), _layer_spec(layer, heads, kv_rank, v_dim),
                  _layer_spec(layer, heads * v_dim, d), _layer_spec(layer, d, d),
                  _layer_spec(layer, 1, d), _stream_spec(pad, ff),
                  _layer_spec(layer, d, 2 * ff), _layer_spec(layer, kconv, ff), _layer_spec(layer, 1, ff),
                  _layer_spec(layer, ff, d), pl.BlockSpec((1, d), lambda g, t: (0, 0))],
        out_specs=tuple(out_specs), out_shape=tuple(out_shape),
        scratch_shapes=[pltpu.VMEM((tile, heads * v_dim), BF16), pltpu.VMEM((pad + tile, ff), F32)],
        compiler_params=_params("parallel", "arbitrary"), name="mix_out",
    )(o_lat, out_a, g_a, g_b, x, wts["w_uv"], wts["w_br_attn"], wts["w_out"], wts["norm_ffn"], conv_state,
      wts["ffn_w_up"], wts["ffn_conv_w"], wts["ffn_conv_b"], wts["ffn_w_down"], norm_final)


def _rope_tables(pos, half):
    inv = ROPE_THETA ** (-jnp.arange(half, dtype=F32) / half)
    ang = pos.astype(F32)[:, None] * inv[None, :]
    c, s = jnp.cos(ang), jnp.sin(ang)
    return jnp.concatenate([c, c, c, c], axis=1), jnp.concatenate([-s, s, -s, s], axis=1)


def _swap_halves(w):
    half = w.shape[-1] // 2
    return jnp.concatenate([w[..., half:], w[..., :half]], axis=-1)


def _pack_weights(p, d, q_rank, kv_rank, heads, nope, rope):
    depth = p["w_in"].shape[0]
    w_in = p["w_in"]
    o_cq = 2 * d
    o_kr = o_cq + q_rank + kv_rank
    o_ga = o_kr + rope
    k_r = w_in[:, :, o_kr:o_ga]
    w_in_p = jnp.concatenate([w_in[:, :, :2 * d], w_in[:, :, o_ga:o_ga + 2 * d], w_in[:, :, o_cq:o_kr],
                              k_r, _swap_halves(k_r)], axis=-1).astype(BF16)
    w_uq = p["w_uq"].reshape(depth, q_rank, heads, nope + rope)
    q_r = w_uq[..., nope:]
    w_uq_p = jnp.concatenate([w_uq[..., :nope].reshape(depth, q_rank, heads * nope),
                              q_r.reshape(depth, q_rank, heads * rope),
                              _swap_halves(q_r).reshape(depth, q_rank, heads * rope)], axis=-1).astype(BF16)
    row = lambda a: a[:, None, :]
    return {
        "norm_mix": row(p["norm_mix"]), "w_in": w_in_p, "q_norm": row(p["q_norm"]), "kv_norm": row(p["kv_norm"]),
        "w_uq": w_uq_p, "w_uk": p["w_uk"].transpose(0, 2, 3, 1).astype(BF16),
        "w_uv": p["w_uv"].transpose(0, 2, 1, 3).astype(BF16),
        "rnn_conv_w": p["rnn_conv_w"], "rnn_conv_b": row(p["rnn_conv_b"]),
        "rnn_wg": jnp.concatenate([p["rnn_wa"], p["rnn_wx"]], axis=-1).astype(BF16),
        "rnn_ba": row(p["rnn_ba"]), "rnn_bx": row(p["rnn_bx"]), "rnn_lambda": row(p["rnn_lambda"]),
        "w_br_rnn": p["w_br_rnn"].astype(BF16), "w_br_attn": p["w_br_attn"].astype(BF16),
        "w_out": p["w_out"].astype(BF16), "norm_ffn": row(p["norm_ffn"]),
        "ffn_w_up": p["ffn_w_up"].astype(BF16), "ffn_conv_w": p["ffn_conv_w"],
        "ffn_conv_b": row(p["ffn_conv_b"]), "ffn_w_down": p["ffn_w_down"].astype(BF16),
    }


def _pad_rows(a, rows):
    return jnp.pad(a, ((0, 0), (rows - a.shape[1], 0), (0, 0)))


def kernel(x_prompt, x_sample, cache_kv_latent, cache_k_rope, state_rnn_h, state_rnn_conv, state_ffn_conv,
           page_table, meta_tokens, norm_mix, w_in, rnn_conv_w, rnn_conv_b, rnn_wa, rnn_ba, rnn_wx, rnn_bx,
           rnn_lambda, q_norm, w_uq, kv_norm, w_uk, w_uv, w_br_rnn, w_br_attn, w_out, norm_ffn, ffn_w_up,
           ffn_conv_w, ffn_conv_b, ffn_w_down, norm_final):
    bp, seq, d = x_prompt.shape
    bs, ts, _ = x_sample.shape
    depth = w_in.shape[0]
    n_meta = meta_tokens.shape[0]
    kv_rank, heads, nope = w_uk.shape[1:]
    q_rank = w_uq.shape[1]
    rope = w_uq.shape[2] // heads - nope
    ff = ffn_w_down.shape[1]
    rnn_k = rnn_conv_w.shape[1]
    ffn_k = ffn_conv_w.shape[1]
    past_len = page_table.shape[1] * cache_kv_latent.shape[2]
    scale = 1.0 / math.sqrt(nope + rope)
    assert 2 * rope == LANES and kv_rank % LANES == 0 and nope % LANES == 0 and d % MXU_WIDTH == 0
    dims = (q_rank, kv_rank, heads, nope, rope)

    wts = _pack_weights(
        dict(norm_mix=norm_mix, w_in=w_in, rnn_conv_w=rnn_conv_w, rnn_conv_b=rnn_conv_b, rnn_wa=rnn_wa,
             rnn_ba=rnn_ba, rnn_wx=rnn_wx, rnn_bx=rnn_bx, rnn_lambda=rnn_lambda, q_norm=q_norm, w_uq=w_uq,
             kv_norm=kv_norm, w_uk=w_uk, w_uv=w_uv, w_br_rnn=w_br_rnn, w_br_attn=w_br_attn, w_out=w_out,
             norm_ffn=norm_ffn, ffn_w_up=ffn_w_up, ffn_conv_w=ffn_conv_w, ffn_conv_b=ffn_conv_b,
             ffn_w_down=ffn_w_down),
        d, q_rank, kv_rank, heads, nope, rope)
    nfin = norm_final[None]

    lp = seq + n_meta
    lpad = _round_up(lp, PROMPT_TILE)
    xp = jnp.concatenate([jnp.broadcast_to(meta_tokens[None].astype(x_prompt.dtype), (bp, n_meta, d)), x_prompt,
                          jnp.zeros((bp, lpad - lp, d), x_prompt.dtype)], axis=1)
    cos_p, sin_p = _rope_tables(jnp.arange(lpad, dtype=jnp.int32), rope // 2)
    zeros_p = lambda c: jnp.zeros((bp, SUBLANES, c), F32)
    rs = ts * bs
    xs = x_sample.transpose(1, 0, 2).reshape(1, rs, d)
    cos_s, sin_s = _rope_tables(past_len + jnp.repeat(jnp.arange(ts, dtype=jnp.int32), bs), rope // 2)
    t_pad = _round_up(ts, 2 * SUBLANES)
    cache_rt = cache_k_rope.transpose(0, 1, 3, 2)

    def time_major(state):
        return state.transpose(1, 0, 2).reshape(1, state.shape[1] * bs, state.shape[2])

    def batch_major(a, k):
        return a.reshape(k, bs, a.shape[-1]).transpose(1, 0, 2)

    outs = {k: [] for k in ("p_lat", "p_rope", "p_h", "p_rc", "p_fc", "s_lat", "s_rope", "s_h", "s_rc", "s_fc")}
    y_p = y_s = None
    for l in range(depth):
        final = l == depth - 1
        out_a, g_a, g_b, ckv, ckvb, kr, krb, qlat, qrope, h_last, x_last = _mix_in(
            xp, cos_p, sin_p, zeros_p(d), jnp.zeros((bp, 1, d), F32), wts, l, PROMPT_TILE, 1, lp, dims)
        o_lat = _prompt_attn(qlat, qrope, ckvb, krb, heads, PROMPT_TILE, ATTN_KEY_TILES, scale, lp)
        res = _mix_out(o_lat, out_a, g_a, g_b, xp, zeros_p(ff), nfin, wts, l, PROMPT_TILE, 1, lp, final)
        xp, u_last = res[0], res[1]
        if final:
            y_p = res[2]
        outs["p_lat"].append(ckv)
        outs["p_rope"].append(kr)
        outs["p_h"].append(h_last[:, 0])
        outs["p_rc"].append(x_last)
        outs["p_fc"].append(u_last)

        rc0 = time_major(state_rnn_conv[l])
        out_a, g_a, g_b, ckv, ckvb, kr, krb, qlat, qrope, h_last, x_last = _mix_in(
            xs, cos_s, sin_s, _pad_rows(rc0, _round_up(rc0.shape[1], SUBLANES)), state_rnn_h[l][None],
            wts, l, rs, bs, ts, dims)
        to_seq = lambda a: jnp.pad(batch_major(a, ts), ((0, 0), (0, t_pad - ts), (0, 0)))
        o_seq = _sample_attn(page_table, batch_major(qlat, ts).reshape(bs, ts * heads, kv_rank),
                             batch_major(qrope, ts).reshape(bs, ts * heads, rope),
                             to_seq(ckv), to_seq(kr), cache_kv_latent, cache_rt, l, heads, scale)
        o_lat = o_seq.reshape(bs, ts, heads * kv_rank).transpose(1, 0, 2).reshape(1, rs, heads * kv_rank)
        fc0 = time_major(state_ffn_conv[l])
        res = _mix_out(o_lat, out_a, g_a, g_b, xs, _pad_rows(fc0, _round_up(fc0.shape[1], SUBLANES)), nfin,
                       wts, l, rs, bs, ts, final)
        xs, u_last = res[0], res[1]
        if final:
            y_s = res[2]
        outs["s_lat"].append(batch_major(ckv, ts))
        outs["s_rope"].append(batch_major(kr, ts))
        outs["s_h"].append(h_last[0])
        outs["s_rc"].append(batch_major(x_last, rnn_k - 1))
        outs["s_fc"].append(batch_major(u_last, ffn_k - 1))

    y_prompt = y_p[:, n_meta:lp]
    y_sample = batch_major(y_s, ts)
    st = {k: jnp.stack(v) for k, v in outs.items()}
    return (y_prompt, y_sample, st["p_lat"], st["p_rope"], st["p_h"], st["p_rc"], st["p_fc"],
            st["s_lat"], st["s_rope"], st["s_h"], st["s_rc"], st["s_fc"])
```
